```python
import jax
import jax.numpy as jnp
from jax import lax
import numpy as np

D_MODEL = 1024
BATCH = 4
SEQ = 4096
DEPTH = 2
DEC_BATCH = 32
DEC_SEQ = 1
PAST_LEN = 8192
PAGE_SIZE = 128

HEAD_DIM = 64
H_FOX = (3 * D_MODEL) // (8 * HEAD_DIM)
H_MLSTM = (3 * D_MODEL) // (8 * HEAD_DIM)
POOL_WINDOWS = (2, 4, 8, 16)
POOL_GROUP = D_MODEL // 16
FOX_W = H_FOX * HEAD_DIM
MLSTM_W = H_MLSTM * HEAD_DIM
POOL_W = len(POOL_WINDOWS) * POOL_GROUP
D_MIX = FOX_W + MLSTM_W + POOL_W
POOL_BUF = max(POOL_WINDOWS) - 1
IN_SPLITS = (FOX_W, FOX_W, FOX_W, MLSTM_W, MLSTM_W, MLSTM_W, MLSTM_W, POOL_W, H_FOX, H_MLSTM, H_MLSTM)
IN_W = sum(IN_SPLITS)
Q_BLOCK = 128
MLSTM_CHUNK = 128
D_FF = ((8 * D_MODEL // 3 + 255) // 256) * 256
N_EXPERTS = 8
TOP_K = 2
E_FF = D_FF
EPS = 1e-6

kernel_name = 'hybrid_fox_mlstm_pool_adaln_step'


def rmsnorm(x, w):
    x32 = x.astype(jnp.float32)
    y = x32 * lax.rsqrt(jnp.mean(x32 * x32, axis=-1, keepdims=True) + EPS)
    return (y * w.astype(jnp.float32)).astype(x.dtype)


def split_in(z):
    offs, acc = [], 0
    for w in IN_SPLITS[:-1]:
        acc += w
        offs.append(acc)
    return jnp.split(z, offs, axis=-1)


def fox_attend(q, k, v, Fq, Fk, pos_q, pos_k):
    s = jnp.einsum('bqhd,bkhd->bhqk', q, k) * (HEAD_DIM ** -0.5)
    s = s + jnp.transpose(Fq, (0, 2, 1))[..., :, None] - jnp.transpose(Fk, (0, 2, 1))[..., None, :]
    s = jnp.where(pos_k[None, :] <= pos_q[:, None], s, -jnp.inf)
    p = jax.nn.softmax(s, axis=-1)
    return jnp.einsum('bhqk,bkhd->bqhd', p, v)


def fox_prompt(q, k, v, lf):
    B, S, H, d = q.shape
    F = jnp.cumsum(lf, axis=1)
    nb = S // Q_BLOCK
    qb = q.reshape(B, nb, Q_BLOCK, H, d).transpose(1, 0, 2, 3, 4)
    Fb = F.reshape(B, nb, Q_BLOCK, H).transpose(1, 0, 2, 3)
    pb = jnp.arange(S).reshape(nb, Q_BLOCK)
    pos_k = jnp.arange(S)
    out = lax.map(lambda a: fox_attend(a[0], k, v, a[1], F, a[2], pos_k), (qb, Fb, pb))
    return out.transpose(1, 0, 2, 3, 4).reshape(B, S, H, d)


def mlstm_chunk(q, k, v, li, lf, C, n, m):
    L = q.shape[1]
    b = jnp.cumsum(lf, axis=1).transpose(0, 2, 1)
    li_t = li.transpose(0, 2, 1)
    causal = jnp.tril(jnp.ones((L, L), dtype=bool))
    D = jnp.where(causal, b[..., :, None] - b[..., None, :] + li_t[..., None, :], -jnp.inf)
    inter = b + m[..., None]
    m_t = jnp.maximum(inter, jnp.max(D, axis=-1))
    a_inter = jnp.exp(inter - m_t)
    S = jnp.einsum('blhd,bshd->bhls', q, k) * jnp.exp(D - m_t[..., None])
    num = a_inter[..., None] * jnp.einsum('blhd,bhde->bhle', q, C) + jnp.einsum('bhls,bshe->bhle', S, v)
    den = a_inter * jnp.einsum('blhd,bhd->bhl', q, n) + jnp.sum(S, axis=-1)
    h = num / jnp.maximum(jnp.abs(den), jnp.exp(-m_t))[..., None]
    b_end = b[..., -1]
    g = b_end[..., None] - b + li_t
    m_new = jnp.maximum(b_end + m, jnp.max(g, axis=-1))
    a_state = jnp.exp(b_end + m - m_new)
    wg = jnp.exp(g - m_new[..., None])
    C_new = a_state[..., None, None] * C + jnp.einsum('bhs,bshd,bshe->bhde', wg, k, v)
    n_new = a_state[..., None] * n + jnp.einsum('bhs,bshd->bhd', wg, k)
    return h.transpose(0, 2, 1, 3), (C_new, n_new, m_new)


def mlstm_prompt(q, k, v, li, lf):
    B, S, H, d = q.shape
    nc = S // MLSTM_CHUNK

    def to_chunks(a):
        return a.reshape((B, nc, MLSTM_CHUNK) + a.shape[2:]).swapaxes(0, 1)

    init = (jnp.zeros((B, H, d, d), q.dtype), jnp.zeros((B, H, d), q.dtype), jnp.zeros((B, H), q.dtype))

    def step(carry, xs):
        h, new = mlstm_chunk(*xs, *carry)
        return new, h

    final, hs = lax.scan(step, init, (to_chunks(q), to_chunks(k), to_chunks(v), to_chunks(li), to_chunks(lf)))
    return hs.swapaxes(0, 1).reshape(B, S, H, d), final


def pool_mix(u_ext, n_prefix, pos0, w_pool, scale):
    B, Lx, _ = u_ext.shape
    T = Lx - n_prefix
    cs = jnp.concatenate([jnp.zeros((B, 1, POOL_W), u_ext.dtype), jnp.cumsum(u_ext, axis=1)], axis=1)
    end = n_prefix + 1 + np.arange(T)
    pos = pos0 + np.arange(T)
    u_new = u_ext[:, n_prefix:]
    diffs = []
    for g, w in enumerate(POOL_WINDOWS):
        sl = slice(g * POOL_GROUP, (g + 1) * POOL_GROUP)
        start = np.maximum(end - w, 0)
        cnt = jnp.asarray(np.minimum(pos + 1, w), jnp.float32)[None, :, None]
        mean = (cs[:, end][..., sl] - cs[:, start][..., sl]) / cnt
        diffs.append(mean - u_new[..., sl])
    d = jnp.stack(diffs, axis=2)
    y = jnp.einsum('btgc,gce->btge', d, w_pool.astype(jnp.float32)).reshape(B, T, POOL_W)
    return y * scale.astype(jnp.float32)


def token_mixers(h, p, past):
    B, T, _ = h.shape
    f32 = jnp.float32
    z = (h @ p['w_in']).astype(f32)
    qf, kf, vf, qm, km, vm, om, u, ff, ig, fg = split_in(z)
    qf = qf.reshape(B, T, H_FOX, HEAD_DIM)
    kf = kf.reshape(B, T, H_FOX, HEAD_DIM)
    vf = vf.reshape(B, T, H_FOX, HEAD_DIM)
    qm = qm.reshape(B, T, H_MLSTM, HEAD_DIM)
    km = km.reshape(B, T, H_MLSTM, HEAD_DIM) * (HEAD_DIM ** -0.5)
    vm = vm.reshape(B, T, H_MLSTM, HEAD_DIM)
    lf_fox = jax.nn.log_sigmoid(ff + p['b_fox_f'].astype(f32))
    li_m = ig + p['b_mlstm_i'].astype(f32)
    lf_m = jax.nn.log_sigmoid(fg + p['b_mlstm_f'].astype(f32))
    if past is None:
        o_fox = fox_prompt(qf, kf, vf, lf_fox)
        h_m, (C, n, m) = mlstm_prompt(qm, km, vm, li_m, lf_m)
        o_pool = pool_mix(u, 0, 0, p['w_pool'], p['pool_scale'])
        buf = u[:, -POOL_BUF:]
    else:
        k_past, v_past, lf_past, C0, n0, m0, buf0 = [a.astype(f32) for a in past]
        p_len = k_past.shape[1]
        k_all = jnp.concatenate([k_past, kf], axis=1)
        v_all = jnp.concatenate([v_past, vf], axis=1)
        F = jnp.cumsum(jnp.concatenate([lf_past, lf_fox], axis=1), axis=1)
        o_fox = fox_attend(qf, k_all, v_all, F[:, p_len:], F, p_len + jnp.arange(T), jnp.arange(p_len + T))
        h_m, (C, n, m) = mlstm_chunk(qm, km, vm, li_m, lf_m, C0, n0, m0)
        u_ext = jnp.concatenate([buf0, u], axis=1)
        o_pool = pool_mix(u_ext, POOL_BUF, p_len, p['w_pool'], p['pool_scale'])
        buf = u_ext[:, -POOL_BUF:]
    h_m = h_m * lax.rsqrt(jnp.mean(h_m * h_m, axis=-1, keepdims=True) + EPS)
    h_m = h_m * p['mlstm_norm_w'].astype(f32).reshape(H_MLSTM, HEAD_DIM)
    o_m = jax.nn.sigmoid(om) * h_m.reshape(B, T, MLSTM_W)
    mix = jnp.concatenate([o_fox.reshape(B, T, FOX_W), o_m, o_pool], axis=-1).astype(h.dtype)
    y = mix @ p['w_out']
    dt = h.dtype
    state = (kf.astype(dt), vf.astype(dt), lf_fox.astype(dt), C.astype(dt), n.astype(dt), m.astype(dt), buf.astype(dt))
    return y, state


def swiglu(h, wg, wu, wd):
    return (jax.nn.silu(h @ wg) * (h @ wu)) @ wd


def moe(h, w_router, wg, wu, wd):
    B, T, D = h.shape
    xt = h.reshape(B * T, D)
    probs = jax.nn.softmax((xt @ w_router).astype(jnp.float32), axis=-1)
    top_p, top_i = lax.top_k(probs, TOP_K)
    top_p = top_p / jnp.sum(top_p, axis=-1, keepdims=True)
    gates = jnp.sum(jax.nn.one_hot(top_i, N_EXPERTS, dtype=jnp.float32) * top_p[..., None], axis=1)
    y = jnp.zeros((B * T, D), jnp.float32)
    for e in range(N_EXPERTS):
        y = y + gates[:, e:e + 1] * swiglu(xt, wg[e], wu[e], wd[e]).astype(jnp.float32)
    return y.reshape(B, T, D)


def layer(x, c, l, P, past):
    mod = jax.nn.silu(c) @ P['w_ada'][l] + P['b_ada'][l]
    sh1, sc1, g1, sh2, sc2, g2 = jnp.split(mod[:, None, :], 6, axis=-1)
    h = rmsnorm(x, P['norm1_w'][l]) * (1 + sc1) + sh1
    names = ('w_in', 'b_fox_f', 'b_mlstm_i', 'b_mlstm_f', 'mlstm_norm_w', 'w_pool', 'pool_scale', 'w_out')
    p = {name: P[name][l] for name in names}
    y, state = token_mixers(h, p, past)
    x = x + (g1 * y).astype(x.dtype)
    h = rmsnorm(x, P['norm2_w'][l]) * (1 + sc2) + sh2
    j = l // 2
    if l % 2 == 0:
        f = swiglu(h, P['w_ffn_gate'][j], P['w_ffn_up'][j], P['w_ffn_down'][j])
    else:
        f = moe(h, P['w_router'][j], P['w_exp_gate'][j], P['w_exp_up'][j], P['w_exp_down'][j])
    x = x + (g2 * f).astype(x.dtype)
    return x, state


def setup_inputs(seed: int = 0) -> dict:
    key = jax.random.key(seed)
    keys = jax.random.split(key, 40)
    f32 = jnp.float32

    def nrm(i, shape, s=1.0):
        return jax.random.normal(keys[i], shape, f32) * s

    n_pages = PAST_LEN // PAGE_SIZE
    n_used = DEC_BATCH * n_pages
    n_pool = n_used + (n_used + 3) // 4
    page_table = jax.random.permutation(keys[0], n_pool)[:n_used].reshape(DEC_BATCH, n_pages).astype(jnp.int32)
    n_dense = (DEPTH + 1) // 2
    n_moe = DEPTH // 2
    D = D_MODEL
    return {
        'x_prompt': nrm(1, (BATCH, SEQ, D)),
        'x_sample': nrm(2, (DEC_BATCH, DEC_SEQ, D)),
        'c_prompt': nrm(3, (BATCH, D)),
        'c_sample': nrm(4, (DEC_BATCH, D)),
        'cache_fox_k': nrm(5, (DEPTH, n_pool, PAGE_SIZE, H_FOX, HEAD_DIM)),
        'cache_fox_v': nrm(6, (DEPTH, n_pool, PAGE_SIZE, H_FOX, HEAD_DIM)),
        'cache_fox_lf': jax.nn.log_sigmoid(3.0 + nrm(7, (DEPTH, n_pool, PAGE_SIZE, H_FOX))),
        'page_table': page_table,
        'state_mlstm_C': nrm(8, (DEPTH, DEC_BATCH, H_MLSTM, HEAD_DIM, HEAD_DIM), 0.1),
        'state_mlstm_n': nrm(9, (DEPTH, DEC_BATCH, H_MLSTM, HEAD_DIM), 0.1),
        'state_mlstm_m': 1.0 + nrm(10, (DEPTH, DEC_BATCH, H_MLSTM), 0.5),
        'state_pool': nrm(11, (DEPTH, DEC_BATCH, POOL_BUF, POOL_W)),
        'w_ada': nrm(12, (DEPTH, D, 6 * D), 0.5 * D ** -0.5),
        'b_ada': nrm(13, (DEPTH, 6 * D), 0.01),
        'norm1_w': 1.0 + nrm(14, (DEPTH, D), 0.01),
        'norm2_w': 1.0 + nrm(15, (DEPTH, D), 0.01),
        'w_in': nrm(16, (DEPTH, D, IN_W), D ** -0.5),
        'b_fox_f': 3.0 + nrm(17, (DEPTH, H_FOX), 0.5),
        'b_mlstm_i': nrm(18, (DEPTH, H_MLSTM), 0.1),
        'b_mlstm_f': jnp.linspace(3.0, 6.0, H_MLSTM, dtype=f32)[None, :] + nrm(19, (DEPTH, H_MLSTM), 0.01),
        'mlstm_norm_w': 1.0 + nrm(20, (DEPTH, MLSTM_W), 0.01),
        'w_pool': nrm(21, (DEPTH, len(POOL_WINDOWS), POOL_GROUP, POOL_GROUP), POOL_GROUP ** -0.5),
        'pool_scale': 1.0 + nrm(22, (DEPTH, POOL_W), 0.1),
        'w_out': nrm(23, (DEPTH, D_MIX, D), D_MIX ** -0.5),
        'w_ffn_gate': nrm(24, (n_dense, D, D_FF), D ** -0.5),
        'w_ffn_up': nrm(25, (n_dense, D, D_FF), D ** -0.5),
        'w_ffn_down': nrm(26, (n_dense, D_FF, D), D_FF ** -0.5),
        'w_router': nrm(27, (n_moe, D, N_EXPERTS), D ** -0.5),
        'w_exp_gate': nrm(28, (n_moe, N_EXPERTS, D, E_FF), D ** -0.5),
        'w_exp_up': nrm(29, (n_moe, N_EXPERTS, D, E_FF), D ** -0.5),
        'w_exp_down': nrm(30, (n_moe, N_EXPERTS, E_FF, D), E_FF ** -0.5),
        'final_norm_w': 1.0 + nrm(31, (D,), 0.01),
    }


def reference(x_prompt, x_sample, c_prompt, c_sample, cache_fox_k, cache_fox_v, cache_fox_lf, page_table,
              state_mlstm_C, state_mlstm_n, state_mlstm_m, state_pool, w_ada, b_ada, norm1_w, norm2_w,
              w_in, b_fox_f, b_mlstm_i, b_mlstm_f, mlstm_norm_w, w_pool, pool_scale, w_out,
              w_ffn_gate, w_ffn_up, w_ffn_down, w_router, w_exp_gate, w_exp_up, w_exp_down, final_norm_w):
    P = {'w_ada': w_ada, 'b_ada': b_ada, 'norm1_w': norm1_w, 'norm2_w': norm2_w, 'w_in': w_in,
         'b_fox_f': b_fox_f, 'b_mlstm_i': b_mlstm_i, 'b_mlstm_f': b_mlstm_f, 'mlstm_norm_w': mlstm_norm_w,
         'w_pool': w_pool, 'pool_scale': pool_scale, 'w_out': w_out, 'w_ffn_gate': w_ffn_gate,
         'w_ffn_up': w_ffn_up, 'w_ffn_down': w_ffn_down, 'w_router': w_router, 'w_exp_gate': w_exp_gate,
         'w_exp_up': w_exp_up, 'w_exp_down': w_exp_down}
    db = x_sample.shape[0]
    xp, xs = x_prompt, x_sample
    st_p, st_s = [], []
    for l in range(DEPTH):
        k_past = cache_fox_k[l][page_table].reshape(db, -1, H_FOX, HEAD_DIM)
        v_past = cache_fox_v[l][page_table].reshape(db, -1, H_FOX, HEAD_DIM)
        lf_past = cache_fox_lf[l][page_table].reshape(db, -1, H_FOX)
        past = (k_past, v_past, lf_past, state_mlstm_C[l], state_mlstm_n[l], state_mlstm_m[l], state_pool[l])
        xp, sp = layer(xp, c_prompt, l, P, None)
        xs, ss = layer(xs, c_sample, l, P, past)
        st_p.append(sp)
        st_s.append(ss)
    y_prompt = rmsnorm(xp, final_norm_w)
    y_sample = rmsnorm(xs, final_norm_w)
    fox_k_p, fox_v_p, fox_lf_p, mlstm_C_p, mlstm_n_p, mlstm_m_p, pool_buf_p = [jnp.stack(a) for a in zip(*st_p)]
    fox_k_s, fox_v_s, fox_lf_s, mlstm_C_s, mlstm_n_s, mlstm_m_s, pool_buf_s = [jnp.stack(a) for a in zip(*st_s)]
    return (y_prompt, y_sample, fox_k_p, fox_v_p, fox_lf_p, mlstm_C_p, mlstm_n_p, mlstm_m_p, pool_buf_p,
            fox_k_s, fox_v_s, fox_lf_s, mlstm_C_s, mlstm_n_s, mlstm_m_s, pool_buf_s)
```

```python
import functools

import jax
import jax.numpy as jnp
from jax import lax
from jax.experimental import pallas as pl
from jax.experimental.pallas import tpu as pltpu

F32 = jnp.float32
BF16 = jnp.bfloat16

HEAD_DIM = 64
LANES = 128
HEADS = 6
MIX_W = HEADS * HEAD_DIM
POOL_WINDOWS = (2, 4, 8, 16)
POOL_W = 256
POOL_HALO = 16
POOL_BUF = 15
MLSTM_CHUNK = 128
N_EXPERTS = 8
EPS = 1e-6
NEG_INF = float("-inf")

_MAIN_W = 7 * MIX_W + POOL_W
_GATE_OFF = _MAIN_W
IN_PAD_W = _MAIN_W + 3 * LANES

VMEM_LIMIT = 56 * 1024 * 1024


def _cparams(sem):
    return pltpu.CompilerParams(dimension_semantics=sem, vmem_limit_bytes=VMEM_LIMIT)


def _sigmoid(x):
    return 1.0 / (1.0 + jnp.exp(-x))


def _log_sigmoid(x):
    return jnp.minimum(x, 0.0) - jnp.log(1.0 + jnp.exp(-jnp.abs(x)))


def _split3(x):
    hi = x.astype(BF16).astype(F32)
    r = x - hi
    mid = r.astype(BF16).astype(F32)
    lo = (r - mid).astype(BF16).astype(F32)
    return hi, mid, lo


def _dot(a, b):
    return jnp.dot(a, b, preferred_element_type=F32)


def _dot_nt(a, b):
    return lax.dot_general(a, b, (((1,), (1,)), ((), ())), preferred_element_type=F32)


def _dot_tn(a, b):
    return lax.dot_general(a, b, (((0,), (0,)), ((), ())), preferred_element_type=F32)


def _tri_dot(tri_bf16, x):
    hi, mid, lo = _split3(x)
    return (_dot(tri_bf16, hi.astype(BF16)) + _dot(tri_bf16, mid.astype(BF16))
            + _dot(tri_bf16, lo.astype(BF16)))


def _dot_tri(x, tri_bf16):
    hi, mid, lo = _split3(x)
    return (_dot(hi.astype(BF16), tri_bf16) + _dot(mid.astype(BF16), tri_bf16)
            + _dot(lo.astype(BF16), tri_bf16))


def _lower_tri(n):
    r = lax.broadcasted_iota(jnp.int32, (n, n), 0)
    c = lax.broadcasted_iota(jnp.int32, (n, n), 1)
    return r >= c


def _modulated_rmsnorm(x, nw, sc, sh):
    y = x * lax.rsqrt(jnp.mean(x * x, axis=-1, keepdims=True) + EPS)
    return (y * nw) * (1.0 + sc) + sh


def _ada_kernel(c_ref, w_ref, b_ref, o_ref):
    c = c_ref[...]
    s = c * _sigmoid(c)
    o_ref[...] = _dot(s.astype(BF16), w_ref[...].astype(BF16)) + b_ref[...]


def _ada_modulation(c_all, w_ada, b_ada):
    depth, d, _ = w_ada.shape
    r = c_all.shape[0]
    return pl.pallas_call(
        _ada_kernel,
        out_shape=jax.ShapeDtypeStruct((depth, 6, r, d), F32),
        grid=(depth, 6),
        in_specs=[
            pl.BlockSpec((r, d), lambda l, j: (0, 0)),
            pl.BlockSpec((None, d, d), lambda l, j: (l, 0, j)),
            pl.BlockSpec((None, 1, d), lambda l, j: (l, 0, j)),
        ],
        out_specs=pl.BlockSpec((None, None, r, d), lambda l, j: (l, j, 0, 0)),
        compiler_params=_cparams(("arbitrary", "arbitrary")),
        name="ada_modulation",
    )(c_all, w_ada, b_ada.reshape(depth, 1, 6 * d))


def _pool_mix(u, halo, pos0, wbd, scale):
    t = u.shape[0]
    ext = jnp.concatenate([halo, u], axis=0)
    p2 = ext + pltpu.roll(ext, 1, 0)
    p4 = p2 + pltpu.roll(p2, 2, 0)
    p8 = p4 + pltpu.roll(p4, 4, 0)
    p16 = p8 + pltpu.roll(p8, 8, 0)
    lane = lax.broadcasted_iota(jnp.int32, (t, POOL_W), 1)
    sums = jnp.where(lane < 64, p2[POOL_HALO:], jnp.where(lane < 128, p4[POOL_HALO:],
                     jnp.where(lane < 192, p8[POOL_HALO:], p16[POOL_HALO:])))
    win = jnp.where(lane < 64, 2.0, jnp.where(lane < 128, 4.0, jnp.where(lane < 192, 8.0, 16.0)))
    pos = (pos0 + lax.broadcasted_iota(jnp.int32, (t, POOL_W), 0)).astype(F32)
    cnt = jnp.minimum(pos + 1.0, win)
    d = sums / cnt - u
    return _dot(d.astype(BF16), wbd) * scale


def _in_proj_prompt_kernel(x_ref, nw_ref, sc_ref, sh_ref, w_ref, gb_ref, wbd_ref, ps_ref,
                           qa_ref, ka_ref, va_ref, kf_ref, vf_ref, lff_ref,
                           qm_ref, km_ref, vm_ref, om_ref, li_ref, lfm_ref, op_ref, ul_ref,
                           fcarry, uhalo):
    t = pl.program_id(1)
    tm = x_ref.shape[0]

    @pl.when(t == 0)
    def _():
        fcarry[...] = jnp.zeros_like(fcarry)
        uhalo[...] = jnp.zeros_like(uhalo)

    hb = _modulated_rmsnorm(x_ref[...], nw_ref[...], sc_ref[...], sh_ref[...]).astype(BF16)

    lane = lax.broadcasted_iota(jnp.int32, (tm, LANES), 1)
    zg = _dot(hb, w_ref[:, _GATE_OFF:IN_PAD_W])
    head_lane = lane < HEADS
    lf_fox = jnp.where(head_lane, _log_sigmoid(zg[:, 0:LANES] + gb_ref[0:1, :]), 0.0)
    li = jnp.where(head_lane, zg[:, LANES:2 * LANES] + gb_ref[1:2, :], 0.0)
    lf_m = jnp.where(head_lane, _log_sigmoid(zg[:, 2 * LANES:3 * LANES] + gb_ref[2:3, :]), 0.0)
    lff_ref[...] = lf_fox
    li_ref[...] = li
    lfm_ref[...] = lf_m

    cum = _tri_dot(_lower_tri(tm).astype(BF16), lf_fox) + fcarry[...]
    fcarry[...] = cum[tm - 1:tm, :]

    zq = _dot(hb, w_ref[:, 0:MIX_W]) * (HEAD_DIM ** -0.5)
    zk = _dot(hb, w_ref[:, MIX_W:2 * MIX_W])
    zv = _dot(hb, w_ref[:, 2 * MIX_W:3 * MIX_W])
    kf_ref[...] = zk
    vf_ref[...] = zv
    low = lane < HEAD_DIM
    for h in range(HEADS):
        p, e = divmod(h, 2)
        sl = slice(p * LANES, (p + 1) * LANES)
        bq, bk, bv = zq[:, sl], zk[:, sl], zv[:, sl]
        if e == 1:
            bq, bk, bv = (pltpu.roll(a, HEAD_DIM, 1) for a in (bq, bk, bv))
        hi, mid, lo = _split3(cum[:, h:h + 1])
        augq = jnp.where(lane == 64, hi, jnp.where(lane == 65, mid, jnp.where(lane == 66, lo,
                         jnp.where(lane < 70, 1.0, 0.0))))
        augk = jnp.where(lane < 67, 1.0, jnp.where(lane == 67, -hi, jnp.where(lane == 68, -mid,
                         jnp.where(lane == 69, -lo, 0.0))))
        augv = jnp.where(lane == 64, 1.0, 0.0)
        qa_ref[h] = jnp.where(low, bq, augq).astype(BF16)
        ka_ref[h] = jnp.where(low, bk, augk).astype(BF16)
        va_ref[h] = jnp.where(low, bv, augv).astype(BF16)

    o = 3 * MIX_W
    qm_ref[...] = _dot(hb, w_ref[:, o:o + MIX_W]).astype(BF16)
    km_ref[...] = (_dot(hb, w_ref[:, o + MIX_W:o + 2 * MIX_W]) * (HEAD_DIM ** -0.5)).astype(BF16)
    vm_ref[...] = _dot(hb, w_ref[:, o + 2 * MIX_W:o + 3 * MIX_W]).astype(BF16)
    om_ref[...] = _dot(hb, w_ref[:, o + 3 * MIX_W:o + 4 * MIX_W])

    u = _dot(hb, w_ref[:, 7 * MIX_W:_MAIN_W])
    op_ref[...] = _pool_mix(u, uhalo[...], t * tm, wbd_ref[...], ps_ref[...]).astype(BF16)
    uhalo[...] = u[tm - POOL_HALO:, :]
    ul_ref[...] = u[tm - POOL_HALO:, :]


def _in_proj_prompt(x, mod, l, nw, w_in_p, gate_b, wbd, pscale, tm):
    b, s, d = x.shape
    nt = s // tm
    row = lambda width, dt: jax.ShapeDtypeStruct((b, s, width), dt)
    head = jax.ShapeDtypeStruct((b, HEADS, s, LANES), BF16)
    tok = lambda width: pl.BlockSpec((None, tm, width), lambda i, t: (i, t, 0))
    hspec = pl.BlockSpec((None, HEADS, tm, LANES), lambda i, t: (i, 0, t, 0))
    modspec = lambda j: pl.BlockSpec((None, None, None, 1, d), lambda i, t: (l, j, i, 0, 0))
    return pl.pallas_call(
        _in_proj_prompt_kernel,
        out_shape=(head, head, head, row(MIX_W, F32), row(MIX_W, F32), row(LANES, F32),
                   row(MIX_W, BF16), row(MIX_W, BF16), row(MIX_W, BF16), row(MIX_W, F32),
                   row(LANES, F32), row(LANES, F32), row(POOL_W, BF16),
                   jax.ShapeDtypeStruct((b, POOL_HALO, POOL_W), F32)),
        grid=(b, nt),
        in_specs=[
            tok(d),
            pl.BlockSpec((None, 1, d), lambda i, t: (l, 0, 0)),
            modspec(1), modspec(0),
            pl.BlockSpec((None, d, IN_PAD_W), lambda i, t: (l, 0, 0)),
            pl.BlockSpec((None, 3, LANES), lambda i, t: (l, 0, 0)),
            pl.BlockSpec((None, POOL_W, POOL_W), lambda i, t: (l, 0, 0)),
            pl.BlockSpec((None, 1, POOL_W), lambda i, t: (l, 0, 0)),
        ],
        out_specs=(hspec, hspec, hspec, tok(MIX_W), tok(MIX_W), tok(LANES),
                   tok(MIX_W), tok(MIX_W), tok(MIX_W), tok(MIX_W), tok(LANES), tok(LANES),
                   tok(POOL_W), pl.BlockSpec((None, POOL_HALO, POOL_W), lambda i, t: (i, 0, 0))),
        scratch_shapes=[pltpu.VMEM((1, LANES), F32), pltpu.VMEM((POOL_HALO, POOL_W), F32)],
        compiler_params=_cparams(("arbitrary", "arbitrary")),
        name="in_proj_prompt",
    )(x, nw, mod, mod, w_in_p, gate_b, wbd, pscale)


def _in_proj_sample_kernel(x_ref, nw_ref, sc_ref, sh_ref, w_ref, gb_ref, z_ref, lff_ref, li_ref, lfm_ref):
    hb = _modulated_rmsnorm(x_ref[...], nw_ref[...], sc_ref[...], sh_ref[...]).astype(BF16)
    z_ref[...] = _dot(hb, w_ref[:, 0:_MAIN_W])
    zg = _dot(hb, w_ref[:, _GATE_OFF:IN_PAD_W])
    lff_ref[...] = _log_sigmoid(zg[:, 0:LANES] + gb_ref[0:1, :])
    li_ref[...] = zg[:, LANES:2 * LANES] + gb_ref[1:2, :]
    lfm_ref[...] = _log_sigmoid(zg[:, 2 * LANES:3 * LANES] + gb_ref[2:3, :])


def _in_proj_sample(x, mod, l, nw, w_in_p, gate_b):
    n, d = x.shape
    gate = jax.ShapeDtypeStruct((n, LANES), F32)
    full = lambda *shape: pl.BlockSpec(shape, lambda i: (0,) * len(shape))
    modspec = lambda j: pl.BlockSpec((None, None, n, d), lambda i: (l, j, 0, 0))
    return pl.pallas_call(
        _in_proj_sample_kernel,
        out_shape=(jax.ShapeDtypeStruct((n, _MAIN_W), F32), gate, gate, gate),
        grid=(1,),
        in_specs=[
            full(n, d),
            pl.BlockSpec((None, 1, d), lambda i: (l, 0, 0)),
            modspec(1), modspec(0),
            pl.BlockSpec((None, d, IN_PAD_W), lambda i: (l, 0, 0)),
            pl.BlockSpec((None, 3, LANES), lambda i: (l, 0, 0)),
        ],
        out_specs=(full(n, _MAIN_W), full(n, LANES), full(n, LANES), full(n, LANES)),
        compiler_params=_cparams(("arbitrary",)),
        name="in_proj_sample",
    )(x, nw, mod, mod, w_in_p, gate_b)


def _fox_prompt_kernel(qa_ref, ka_ref, va_ref, o_ref, *, tq):
    qi = pl.program_id(2)
    q = (qa_ref[0], qa_ref[1])

    def block(e, ki, carry, diag):
        m, acc = carry
        k = ka_ref[e, pl.ds(pl.multiple_of(ki * tq, tq), tq), :]
        v = va_ref[e, pl.ds(pl.multiple_of(ki * tq, tq), tq), :]
        s = _dot_nt(q[e], k)
        if diag:
            s = jnp.where(_lower_tri(tq), s, NEG_INF)
        m_new = jnp.maximum(m, jnp.max(s, axis=-1, keepdims=True))
        p = jnp.exp(s - m_new)
        acc = acc * jnp.exp(m - m_new) + _dot(p.astype(BF16), v)
        return m_new, acc

    def body(ki, carry):
        return block(0, ki, carry[0], False), block(1, ki, carry[1], False)

    init = (jnp.full((tq, 1), NEG_INF, F32), jnp.zeros((tq, LANES), F32))
    carry = lax.fori_loop(0, qi, body, (init, init))
    outs = []
    for e in range(2):
        _, acc = block(e, qi, carry[e], True)
        outs.append(acc / acc[:, HEAD_DIM:HEAD_DIM + 1])
    lane = lax.broadcasted_iota(jnp.int32, (tq, LANES), 1)
    o_ref[...] = jnp.where(lane < HEAD_DIM, outs[0], pltpu.roll(outs[1], HEAD_DIM, 1)).astype(o_ref.dtype)


def _fox_prompt(qa, ka, va, tq):
    b, _, s, _ = qa.shape
    return pl.pallas_call(
        functools.partial(_fox_prompt_kernel, tq=tq),
        out_shape=jax.ShapeDtypeStruct((b, s, MIX_W), BF16),
        grid=(b, HEADS // 2, s // tq),
        in_specs=[
            pl.BlockSpec((None, 2, tq, LANES), lambda i, p, t: (i, p, t, 0)),
            pl.BlockSpec((None, 2, s, LANES), lambda i, p, t: (i, p, 0, 0)),
            pl.BlockSpec((None, 2, s, LANES), lambda i, p, t: (i, p, 0, 0)),
        ],
        out_specs=pl.BlockSpec((None, tq, LANES), lambda i, p, t: (i, t, p)),
        compiler_params=_cparams(("arbitrary", "arbitrary", "arbitrary")),
        name="fox_prompt",
    )(qa, ka, va)


def _mlstm_prompt_kernel(q_ref, k_ref, v_ref, og_ref, li_ref, lf_ref, nw_ref,
                         o_ref, c_out, n_out, m_out, c_st, n_st, m_st):
    c = pl.program_id(1)
    nb, L, _ = q_ref.shape

    @pl.when(c == 0)
    def _():
        c_st[...] = jnp.zeros_like(c_st)
        n_st[...] = jnp.zeros_like(n_st)
        m_st[...] = jnp.zeros_like(m_st)

    tril = _lower_tri(L)
    tri = tril.astype(BF16)
    lane = lax.broadcasted_iota(jnp.int32, (L, LANES), 1)
    lane1 = lax.broadcasted_iota(jnp.int32, (1, LANES), 1)
    rowi = lax.broadcasted_iota(jnp.int32, (LANES, LANES), 0)
    coli = lax.broadcasted_iota(jnp.int32, (LANES, LANES), 1)
    same_head = (rowi < HEAD_DIM) == (coli < HEAD_DIM)

    for bi in range(nb):
        lf = lf_ref[bi]
        li = li_ref[bi]
        bcum = _tri_dot(tri, lf)
        m_prev = m_st[bi]
        b_end = bcum[L - 1:L, :]
        g = b_end - bcum + li
        m_new = jnp.maximum(b_end + m_prev, jnp.max(g, axis=0, keepdims=True))
        a_state = jnp.exp(b_end + m_prev - m_new)
        wg = jnp.exp(g - m_new)
        inter = bcum + m_prev
        bcum_t = bcum.T
        li_t = li.T
        for p in range(HEADS // 2):
            sl = slice(p * LANES, (p + 1) * LANES)
            q = q_ref[bi, :, sl]
            k = k_ref[bi, :, sl]
            v = v_ref[bi, :, sl]
            cp = c_st[bi, p]
            npair = n_st[bi, p]
            qc = _dot(q, cp.astype(BF16))
            qn_full = q.astype(F32) * npair
            hpair = jnp.zeros((L, LANES), F32)
            for e in range(2):
                h = 2 * p + e
                mine = (lane < HEAD_DIM) if e == 0 else (lane >= HEAD_DIM)
                dmat = jnp.where(tril, bcum[:, h:h + 1] - bcum_t[h:h + 1, :] + li_t[h:h + 1, :], NEG_INF)
                int_h = inter[:, h:h + 1]
                m_t = jnp.maximum(int_h, jnp.max(dmat, axis=-1, keepdims=True))
                a_int = jnp.exp(int_h - m_t)
                s = _dot_nt(jnp.where(mine, q, jnp.zeros_like(q)), k) * jnp.exp(dmat - m_t)
                sv = _dot(s.astype(BF16), v)
                den = a_int * jnp.sum(jnp.where(mine, qn_full, 0.0), axis=-1, keepdims=True) \
                    + jnp.sum(s, axis=-1, keepdims=True)
                hh = (a_int * qc + sv) / jnp.maximum(jnp.abs(den), jnp.exp(-m_t))
                ms = jnp.sum(jnp.where(mine, hh * hh, 0.0), axis=-1, keepdims=True) * (1.0 / HEAD_DIM)
                hpair = jnp.where(mine, hh * lax.rsqrt(ms + EPS), hpair)
            o_ref[bi, :, sl] = (_sigmoid(og_ref[bi, :, sl]) * (hpair * nw_ref[:, sl])).astype(o_ref.dtype)
            wgp = jnp.where(lane < HEAD_DIM, wg[:, 2 * p:2 * p + 1], wg[:, 2 * p + 1:2 * p + 2])
            ap = jnp.where(lane1 < HEAD_DIM, a_state[:, 2 * p:2 * p + 1], a_state[:, 2 * p + 1:2 * p + 2])
            kv = _dot_tn(k, (v.astype(F32) * wgp).astype(BF16))
            c_st[bi, p] = cp * ap + jnp.where(same_head, kv, 0.0)
            n_st[bi, p] = npair * ap + jnp.sum(k.astype(F32) * wgp, axis=0, keepdims=True)
        m_st[bi] = m_new

    @pl.when(c == pl.num_programs(1) - 1)
    def _():
        c_out[...] = c_st[...]
        n_out[...] = n_st[...]
        m_out[...] = m_st[...]


def _mlstm_prompt(qm, km, vm, om, li, lfm, nw, l, nb):
    b, s, _ = qm.shape
    L = MLSTM_CHUNK
    np_ = HEADS // 2
    tok = lambda w: pl.BlockSpec((nb, L, w), lambda i, c: (i, c, 0))
    return pl.pallas_call(
        _mlstm_prompt_kernel,
        out_shape=(jax.ShapeDtypeStruct((b, s, MIX_W), BF16),
                   jax.ShapeDtypeStruct((b, np_, LANES, LANES), F32),
                   jax.ShapeDtypeStruct((b, np_, 1, LANES), F32),
                   jax.ShapeDtypeStruct((b, 1, LANES), F32)),
        grid=(b // nb, s // L),
        in_specs=[tok(MIX_W), tok(MIX_W), tok(MIX_W), tok(MIX_W), tok(LANES), tok(LANES),
                  pl.BlockSpec((None, 1, MIX_W), lambda i, c: (l, 0, 0))],
        out_specs=(tok(MIX_W),
                   pl.BlockSpec((nb, np_, LANES, LANES), lambda i, c: (i, 0, 0, 0)),
                   pl.BlockSpec((nb, np_, 1, LANES), lambda i, c: (i, 0, 0, 0)),
                   pl.BlockSpec((nb, 1, LANES), lambda i, c: (i, 0, 0))),
        scratch_shapes=[pltpu.VMEM((nb, np_, LANES, LANES), F32),
                        pltpu.VMEM((nb, np_, 1, LANES), F32),
                        pltpu.VMEM((nb, 1, LANES), F32)],
        compiler_params=_cparams(("arbitrary", "arbitrary")),
        name="mlstm_prompt",
    )(qm, km, vm, om, li, lfm, nw)


def _out_proj_kernel(of_ref, om_ref, op_ref, x_ref, g_ref, w_ref, o_ref):
    y = (_dot(of_ref[...], w_ref[0:MIX_W, :]) + _dot(om_ref[...], w_ref[MIX_W:2 * MIX_W, :])
         + _dot(op_ref[...], w_ref[2 * MIX_W:, :]))
    o_ref[...] = x_ref[...] + g_ref[...] * y


def _out_proj(of, om, op, x, mod, l, w_out_b, tm, per_row_mod):
    b, s, d = x.shape
    tok = lambda w: pl.BlockSpec((None, tm, w), lambda i, t: (i, t, 0))
    if per_row_mod:
        gspec = pl.BlockSpec((None, None, tm, d), lambda i, t: (l, 2, 0, 0))
    else:
        gspec = pl.BlockSpec((None, None, None, 1, d), lambda i, t: (l, 2, i, 0, 0))
    return pl.pallas_call(
        _out_proj_kernel,
        out_shape=jax.ShapeDtypeStruct((b, s, d), F32),
        grid=(b, s // tm),
        in_specs=[tok(MIX_W), tok(MIX_W), tok(POOL_W), tok(d), gspec,
                  pl.BlockSpec((None, d, d), lambda i, t: (l, 0, 0))],
        out_specs=tok(d),
        compiler_params=_cparams(("arbitrary", "arbitrary")),
        name="out_proj",
    )(of, om, op, x, mod, w_out_b)


def _final_norm(x, fw):
    return (x * lax.rsqrt(jnp.mean(x * x, axis=-1, keepdims=True) + EPS)) * fw


def _ffn_kernel(x_ref, nw_ref, sc_ref, sh_ref, g_ref, wg_ref, wu_ref, wd_ref, fw_ref, o_ref,
                h_sc, acc, *, final):
    j = pl.program_id(2)

    @pl.when(j == 0)
    def _():
        h_sc[...] = _modulated_rmsnorm(x_ref[...], nw_ref[...], sc_ref[...], sh_ref[...]).astype(BF16)
        acc[...] = jnp.zeros_like(acc)

    h = h_sc[...]
    a = _dot(h, wg_ref[...])
    u = _dot(h, wu_ref[...])
    acc[...] += _dot(((a * _sigmoid(a)) * u).astype(BF16), wd_ref[...])

    @pl.when(j == pl.num_programs(2) - 1)
    def _():
        y = x_ref[...] + g_ref[...] * acc[...]
        o_ref[...] = _final_norm(y, fw_ref[...]) if final else y


def _mod_specs(l, d, tm, per_row_mod, nidx):
    def spec(j):
        if per_row_mod:
            return pl.BlockSpec((None, None, tm, d), lambda *a: (l, j, 0, 0))
        return pl.BlockSpec((None, None, None, 1, d), lambda *a: (l, j, a[0], 0, 0))
    return spec(4), spec(3), spec(5)


def _ffn(x, mod, l, nw, wg, wu, wd, fw, tm, tf, per_row_mod, final):
    b, s, d = x.shape
    ff = wg.shape[-1]
    sc, sh, g = _mod_specs(l, d, tm, per_row_mod, 3)
    return pl.pallas_call(
        functools.partial(_ffn_kernel, final=final),
        out_shape=jax.ShapeDtypeStruct((b, s, d), F32),
        grid=(b, s // tm, ff // tf),
        in_specs=[
            pl.BlockSpec((None, tm, d), lambda i, t, j: (i, t, 0)),
            pl.BlockSpec((None, 1, d), lambda i, t, j: (l, 0, 0)),
            sc, sh, g,
            pl.BlockSpec((d, tf), lambda i, t, j: (0, j)),
            pl.BlockSpec((d, tf), lambda i, t, j: (0, j)),
            pl.BlockSpec((tf, d), lambda i, t, j: (j, 0)),
            pl.BlockSpec((1, d), lambda i, t, j: (0, 0)),
        ],
        out_specs=pl.BlockSpec((None, tm, d), lambda i, t, j: (i, t, 0)),
        scratch_shapes=[pltpu.VMEM((tm, d), BF16), pltpu.VMEM((tm, d), F32)],
        compiler_params=_cparams(("arbitrary", "arbitrary", "arbitrary")),
        name="ffn_swiglu",
    )(x, nw, mod, mod, mod, wg, wu, wd, fw)


def _top2_gates(logits):
    lane = lax.broadcasted_iota(jnp.int32, logits.shape, 1)
    valid = lane < N_EXPERTS
    z = jnp.where(valid, logits, NEG_INF)
    pz = jnp.exp(z - jnp.max(z, axis=-1, keepdims=True))
    probs = pz / jnp.sum(pz, axis=-1, keepdims=True)
    m1 = jnp.max(probs, axis=-1, keepdims=True)
    i1 = jnp.min(jnp.where(probs == m1, lane, LANES), axis=-1, keepdims=True)
    rest = jnp.where((lane == i1) | (~valid), -1.0, probs)
    m2 = jnp.max(rest, axis=-1, keepdims=True)
    i2 = jnp.min(jnp.where(rest == m2, lane, LANES), axis=-1, keepdims=True)
    top = jnp.where(lane == i1, m1, jnp.where(lane == i2, m2, 0.0))
    return top / (m1 + m2)


def _moe_dense_kernel(x_ref, nw_ref, sc_ref, sh_ref, g_ref, wr_ref, wg_ref, wu_ref, wd_ref, fw_ref,
                      o_ref, h_sc, gates, acc, *, final):
    e = pl.program_id(2)
    j = pl.program_id(3)

    @pl.when((e == 0) & (j == 0))
    def _():
        hb = _modulated_rmsnorm(x_ref[...], nw_ref[...], sc_ref[...], sh_ref[...]).astype(BF16)
        h_sc[...] = hb
        gates[...] = _top2_gates(_dot(hb, wr_ref[...]))
        acc[...] = jnp.zeros_like(acc)

    h = h_sc[...]
    a = _dot(h, wg_ref[...])
    u = _dot(h, wu_ref[...])
    lane = lax.broadcasted_iota(jnp.int32, gates.shape, 1)
    ge = jnp.sum(jnp.where(lane == e, gates[...], 0.0), axis=-1, keepdims=True)
    acc[...] += ge * _dot(((a * _sigmoid(a)) * u).astype(BF16), wd_ref[...])

    @pl.when((e == pl.num_programs(2) - 1) & (j == pl.num_programs(3) - 1))
    def _():
        y = x_ref[...] + g_ref[...] * acc[...]
        o_ref[...] = _final_norm(y, fw_ref[...]) if final else y


def _moe_dense(x, mod, l, nw, wr, wg, wu, wd, fw, tm, tf, per_row_mod, final):
    b, s, d = x.shape
    ne, _, ff = wg.shape
    sc, sh, g = _mod_specs(l, d, tm, per_row_mod, 4)
    return pl.pallas_call(
        functools.partial(_moe_dense_kernel, final=final),
        out_shape=jax.ShapeDtypeStruct((b, s, d), F32),
        grid=(b, s // tm, ne, ff // tf),
        in_specs=[
            pl.BlockSpec((None, tm, d), lambda i, t, e, j: (i, t, 0)),
            pl.BlockSpec((None, 1, d), lambda i, t, e, j: (l, 0, 0)),
            sc, sh, g,
            pl.BlockSpec((d, LANES), lambda i, t, e, j: (0, 0)),
            pl.BlockSpec((None, d, tf), lambda i, t, e, j: (e, 0, j)),
            pl.BlockSpec((None, d, tf), lambda i, t, e, j: (e, 0, j)),
            pl.BlockSpec((None, tf, d), lambda i, t, e, j: (e, j, 0)),
            pl.BlockSpec((1, d), lambda i, t, e, j: (0, 0)),
        ],
        out_specs=pl.BlockSpec((None, tm, d), lambda i, t, e, j: (i, t, 0)),
        scratch_shapes=[pltpu.VMEM((tm, d), BF16), pltpu.VMEM((tm, LANES), F32), pltpu.VMEM((tm, d), F32)],
        compiler_params=_cparams(("arbitrary",) * 4),
        name="moe_dense",
    )(x, nw, mod, mod, mod, wr, wg, wu, wd, fw)


def _page_cumsum_kernel(lf_ref, o_ref):
    n = lf_ref.shape[-1]
    r = lax.broadcasted_iota(jnp.int32, (n, n), 0)
    c = lax.broadcasted_iota(jnp.int32, (n, n), 1)
    o_ref[...] = _dot_tri(lf_ref[...], (r <= c).astype(BF16))


def _page_cumsum(lf_t):
    depth, h, n_pool, page = lf_t.shape
    return pl.pallas_call(
        _page_cumsum_kernel,
        out_shape=jax.ShapeDtypeStruct(lf_t.shape, F32),
        grid=(depth, h),
        in_specs=[pl.BlockSpec((None, None, n_pool, page), lambda l, i: (l, i, 0, 0))],
        out_specs=pl.BlockSpec((None, None, n_pool, page), lambda l, i: (l, i, 0, 0)),
        compiler_params=_cparams(("arbitrary", "arbitrary")),
        name="page_cumsum",
    )(lf_t)


def _fox_decode_kernel(pt_ref, q_ref, kn_ref, vn_ref, lfn_ref, *refs, pps, ns):
    k_refs = refs[0:pps]
    f_refs = refs[pps:2 * pps]
    v_refs = refs[2 * pps:3 * pps]
    o_ref = refs[3 * pps]
    s_sc, acc, carry, l_sc, pn_sc = refs[3 * pps + 1:]
    b = pl.program_id(0)
    step = pl.program_id(1)
    page = k_refs[0].shape[-1]
    q3 = q_ref[...].reshape(HEADS, HEAD_DIM, 1) * (HEAD_DIM ** -0.5)

    @pl.when(step == 0)
    def _():
        carry[...] = jnp.zeros_like(carry)

    @pl.when(step < ns)
    def _():
        run = carry[...]
        for r in range(pps):
            j = step * pps + r
            row = pt_ref[b, j] % 8
            floc = f_refs[r][:, pl.ds(row, 1), :]
            sc = jnp.sum(k_refs[r][...] * q3, axis=1, keepdims=True)
            s_sc[:, :, pl.ds(pl.multiple_of(j * page, page), page)] = sc - (run + floc)
            run = run + floc[:, :, page - 1:page]
        carry[...] = run

    @pl.when(step == ns)
    def _():
        s_all = s_sc[...]
        s_new = jnp.sum(q3 * kn_ref[...].reshape(HEADS, HEAD_DIM, 1), axis=1, keepdims=True) \
            - (carry[...] + lfn_ref[...])
        m = jnp.maximum(jnp.max(s_all, axis=2, keepdims=True), s_new)
        p = jnp.exp(s_all - m)
        p_new = jnp.exp(s_new - m)
        s_sc[...] = p
        l_sc[...] = jnp.sum(p, axis=2, keepdims=True) + p_new
        pn_sc[...] = p_new
        acc[...] = jnp.zeros_like(acc)

    @pl.when(step >= ns)
    def _():
        a = acc[...]
        for r in range(pps):
            j = (step - ns) * pps + r
            pj = s_sc[:, :, pl.ds(pl.multiple_of(j * page, page), page)]
            a = a + v_refs[r][...] * pj
        acc[...] = a

    @pl.when(step == 2 * ns - 1)
    def _():
        o3 = jnp.sum(acc[...], axis=2, keepdims=True) + pn_sc[...] * vn_ref[...].reshape(HEADS, HEAD_DIM, 1)
        o_ref[...] = (o3 / l_sc[...]).reshape(MIX_W, 1)


def _fox_decode(page_table, q_col, kn_col, vn_col, lfn, k_t, v_t, floc, l, pps):
    nb, n_pages = page_table.shape
    page = k_t.shape[-1]
    ns = n_pages // pps
    col = pl.BlockSpec((None, MIX_W, 1), lambda i, s, pt: (i, 0, 0))

    def kspec(r):
        return pl.BlockSpec((None, None, HEADS, HEAD_DIM, page),
                            lambda i, s, pt: (l, pt[i, jnp.minimum(s, ns - 1) * pps + r], 0, 0, 0))

    def fspec(r):
        return pl.BlockSpec((None, HEADS, None, 8, page),
                            lambda i, s, pt: (l, 0, pt[i, jnp.minimum(s, ns - 1) * pps + r] // 8, 0, 0))

    def vspec(r):
        return pl.BlockSpec((None, None, HEADS, HEAD_DIM, page),
                            lambda i, s, pt: (l, pt[i, jnp.maximum(s - ns, 0) * pps + r], 0, 0, 0))

    grid_spec = pltpu.PrefetchScalarGridSpec(
        num_scalar_prefetch=1,
        grid=(nb, 2 * ns),
        in_specs=[col, col, col, pl.BlockSpec((None, HEADS, 1, 1), lambda i, s, pt: (i, 0, 0, 0))]
        + [kspec(r) for r in range(pps)] + [fspec(r) for r in range(pps)] + [vspec(r) for r in range(pps)],
        out_specs=col,
        scratch_shapes=[pltpu.VMEM((HEADS, 1, n_pages * page), F32),
                        pltpu.VMEM((HEADS, HEAD_DIM, page), F32),
                        pltpu.VMEM((HEADS, 1, 1), F32),
                        pltpu.VMEM((HEADS, 1, 1), F32),
                        pltpu.VMEM((HEADS, 1, 1), F32)],
    )
    return pl.pallas_call(
        functools.partial(_fox_decode_kernel, pps=pps, ns=ns),
        out_shape=jax.ShapeDtypeStruct((nb, MIX_W, 1), F32),
        grid_spec=grid_spec,
        compiler_params=_cparams(("arbitrary", "arbitrary")),
        name="fox_decode",
    )(page_table, q_col, kn_col, vn_col, lfn, *([k_t] * pps), *([floc] * pps), *([v_t] * pps))


def _mlstm_decode_kernel(q_ref, k_ref, v_ref, og_ref, li_ref, lf_ref, c_ref, n_ref, m_ref, nw_ref,
                         o_ref, c_out, n_out, m_out):
    q = q_ref[...]
    k = k_ref[...] * (HEAD_DIM ** -0.5)
    v = v_ref[...]
    c = c_ref[...]
    n = n_ref[...]
    m = m_ref[...]
    li = li_ref[...]
    lf = lf_ref[...]
    inter = lf + m
    m_t = jnp.maximum(inter, li)
    a_int = jnp.exp(inter - m_t)
    w_in = jnp.exp(li - m_t)
    s = jnp.sum(q * k, axis=1, keepdims=True) * w_in
    num = a_int * jnp.sum(q * c, axis=1, keepdims=True) + s * v
    den = a_int * jnp.sum(q * n, axis=1, keepdims=True) + s
    h = num / jnp.maximum(jnp.abs(den), jnp.exp(-m_t))
    h = h * lax.rsqrt(jnp.mean(h * h, axis=-1, keepdims=True) + EPS)
    o_ref[...] = _sigmoid(og_ref[...]) * (h * nw_ref[...])
    c_out[...] = a_int * c + w_in * (k * v)
    n_out[...] = a_int * n + w_in * k
    m_out[...] = m_t


def _mlstm_decode(q_col, k_col, v_row, og_row, li, lf, c0, n0_col, m0, nw_row):
    nb = q_col.shape[0]
    d = HEAD_DIM
    colspec = pl.BlockSpec((None, HEADS, d, 1), lambda i: (i, 0, 0, 0))
    rowspec = pl.BlockSpec((None, HEADS, 1, d), lambda i: (i, 0, 0, 0))
    sclspec = pl.BlockSpec((None, HEADS, 1, 1), lambda i: (i, 0, 0, 0))
    matspec = pl.BlockSpec((None, HEADS, d, d), lambda i: (i, 0, 0, 0))
    return pl.pallas_call(
        _mlstm_decode_kernel,
        out_shape=(jax.ShapeDtypeStruct((nb, HEADS, 1, d), F32),
                   jax.ShapeDtypeStruct((nb, HEADS, d, d), F32),
                   jax.ShapeDtypeStruct((nb, HEADS, d, 1), F32),
                   jax.ShapeDtypeStruct((nb, HEADS, 1, 1), F32)),
        grid=(nb,),
        in_specs=[colspec, colspec, rowspec, rowspec, sclspec, sclspec, matspec, colspec, sclspec,
                  pl.BlockSpec((HEADS, 1, d), lambda i: (0, 0, 0))],
        out_specs=(rowspec, matspec, colspec, sclspec),
        compiler_params=_cparams(("arbitrary",)),
        name="mlstm_decode",
    )(q_col, k_col, v_row, og_row, li, lf, c0, n0_col, m0, nw_row)


def _pool_decode_kernel(buf_ref, u_ref, wbd_ref, ps_ref, o_ref, nb_ref, *, pos):
    u = u_ref[...]
    n = buf_ref.shape[0]
    lane = lax.broadcasted_iota(jnp.int32, u.shape, 1)
    sums = {}
    run = u
    for back in range(1, max(POOL_WINDOWS)):
        run = run + buf_ref[n - back]
        if back + 1 in POOL_WINDOWS:
            sums[back + 1] = run
    cnt = {w: float(min(pos + 1, w)) for w in POOL_WINDOWS}
    mean = jnp.where(lane < 64, sums[2] / cnt[2], jnp.where(lane < 128, sums[4] / cnt[4],
                     jnp.where(lane < 192, sums[8] / cnt[8], sums[16] / cnt[16])))
    o_ref[...] = _dot((mean - u).astype(BF16), wbd_ref[...]) * ps_ref[...]
    for r in range(n - 1):
        nb_ref[r] = buf_ref[r + 1]
    nb_ref[n - 1] = u


def _pool_decode(buf_t, u, wbd, pscale, l, pos):
    _, n, nb, w = buf_t.shape
    return pl.pallas_call(
        functools.partial(_pool_decode_kernel, pos=pos),
        out_shape=(jax.ShapeDtypeStruct((nb, w), F32), jax.ShapeDtypeStruct((n, nb, w), F32)),
        grid=(1,),
        in_specs=[pl.BlockSpec((None, n, nb, w), lambda i: (l, 0, 0, 0)),
                  pl.BlockSpec((nb, w), lambda i: (0, 0)),
                  pl.BlockSpec((None, w, w), lambda i: (l, 0, 0)),
                  pl.BlockSpec((None, 1, w), lambda i: (l, 0, 0))],
        out_specs=(pl.BlockSpec((nb, w), lambda i: (0, 0)),
                   pl.BlockSpec((n, nb, w), lambda i: (0, 0, 0))),
        compiler_params=_cparams(("arbitrary",)),
        name="pool_decode",
    )(buf_t, u, wbd, pscale)


def _tile(n, pref):
    t = min(n, pref)
    assert n % t == 0, (n, t)
    return t


def kernel(x_prompt, x_sample, c_prompt, c_sample, cache_fox_k, cache_fox_v, cache_fox_lf, page_table,
           state_mlstm_C, state_mlstm_n, state_mlstm_m, state_pool, w_ada, b_ada, norm1_w, norm2_w,
           w_in, b_fox_f, b_mlstm_i, b_mlstm_f, mlstm_norm_w, w_pool, pool_scale, w_out,
           w_ffn_gate, w_ffn_up, w_ffn_down, w_router, w_exp_gate, w_exp_up, w_exp_down, final_norm_w):
    depth = w_in.shape[0]
    b, s, d = x_prompt.shape
    nb = x_sample.shape[0]
    n_pool, page = cache_fox_k.shape[1], cache_fox_k.shape[2]
    assert x_sample.shape[1] == 1 and s % MLSTM_CHUNK == 0 and n_pool % 8 == 0

    pad_gate = lambda w: jnp.pad(w, ((0, 0), (0, 0), (0, LANES - HEADS)))
    w_in_p = jnp.concatenate(
        [w_in[:, :, :_MAIN_W], pad_gate(w_in[:, :, _MAIN_W:_MAIN_W + HEADS]),
         pad_gate(w_in[:, :, _MAIN_W + HEADS:_MAIN_W + 2 * HEADS]),
         pad_gate(w_in[:, :, _MAIN_W + 2 * HEADS:])], axis=-1).astype(BF16)
    gate_b = jnp.pad(jnp.stack([b_fox_f, b_mlstm_i, b_mlstm_f], axis=1), ((0, 0), (0, 0), (0, LANES - HEADS)))
    wbd = jnp.zeros((depth, POOL_W, POOL_W), F32)
    for g in range(len(POOL_WINDOWS)):
        sl = slice(g * HEAD_DIM, (g + 1) * HEAD_DIM)
        wbd = wbd.at[:, sl, sl].set(w_pool[:, g])
    wbd = wbd.astype(BF16)
    pscale = pool_scale.reshape(depth, 1, POOL_W)
    w_out_b = w_out.astype(BF16)
    nw1 = norm1_w.reshape(depth, 1, d)
    nw2 = norm2_w.reshape(depth, 1, d)
    mnw = mlstm_norm_w.reshape(depth, 1, MIX_W)
    mnw_row = mlstm_norm_w.reshape(depth, HEADS, 1, HEAD_DIM)
    fw = final_norm_w.reshape(1, d)
    w_router_p = jnp.pad(w_router, ((0, 0), (0, 0), (0, LANES - N_EXPERTS))).astype(BF16)
    wfg, wfu, wfd = (w.astype(BF16) for w in (w_ffn_gate, w_ffn_up, w_ffn_down))
    weg, weu, wed = (w.astype(BF16) for w in (w_exp_gate, w_exp_up, w_exp_down))

    k_t = jnp.transpose(cache_fox_k, (0, 1, 3, 4, 2))
    v_t = jnp.transpose(cache_fox_v, (0, 1, 3, 4, 2))
    lf_t = jnp.transpose(cache_fox_lf, (0, 3, 1, 2))
    floc = _page_cumsum(lf_t).reshape(depth, HEADS, n_pool // 8, 8, page)
    pool_t = jnp.transpose(state_pool, (0, 2, 1, 3))

    mod = _ada_modulation(jnp.concatenate([c_prompt, c_sample], axis=0), w_ada, b_ada)
    mod_p = mod[:, :, :b].reshape(depth, 6, b, 1, d)
    mod_s = mod[:, :, b:]

    tm = _tile(s, 512)
    tq = _tile(s, 512)
    tf = w_ffn_gate.shape[-1] // 2
    xp = x_prompt
    xs = x_sample.reshape(1, nb, d)
    st_p, st_s = [], []
    for l in range(depth):
        last = l == depth - 1
        (qa, ka, va, kf, vf, lff, qm, km, vm, om, li, lfm, opool, ulast) = _in_proj_prompt(
            xp, mod_p, l, nw1, w_in_p, gate_b, wbd, pscale, tm)
        o_fox = _fox_prompt(qa, ka, va, tq)
        o_m, c_pair, n_pair, m_fin = _mlstm_prompt(qm, km, vm, om, li, lfm, mnw, l, 2 if b % 2 == 0 else 1)
        x1 = _out_proj(o_fox, o_m, opool, xp, mod_p, l, w_out_b, tm, False)
        if l % 2 == 0:
            j = l // 2
            xp = _ffn(x1, mod_p, l, nw2, wfg[j], wfu[j], wfd[j], fw, tm, tf, False, last)
        else:
            j = l // 2
            xp = _moe_dense(x1, mod_p, l, nw2, w_router_p[j], weg[j], weu[j], wed[j], fw, tm, tf, False, last)
        c_fin = jnp.stack([c_pair[:, p, e * HEAD_DIM:(e + 1) * HEAD_DIM, e * HEAD_DIM:(e + 1) * HEAD_DIM]
                           for p in range(HEADS // 2) for e in range(2)], axis=1)
        st_p.append((kf.reshape(b, s, HEADS, HEAD_DIM), vf.reshape(b, s, HEADS, HEAD_DIM), lff[:, :, :HEADS],
                     c_fin, n_pair.reshape(b, HEADS, HEAD_DIM), m_fin[:, 0, :HEADS],
                     ulast[:, POOL_HALO - POOL_BUF:]))

        z, lff_s, li_s, lfm_s = _in_proj_sample(xs[0], mod_s, l, nw1, w_in_p, gate_b)
        seg = lambda i: z[:, i * MIX_W:(i + 1) * MIX_W]
        qf_s, kf_s, vf_s, qm_s, km_s, vm_s, om_s = (seg(i) for i in range(7))
        u_s = z[:, 7 * MIX_W:]
        col = lambda a: a.reshape(nb, MIX_W, 1)
        o_fox_s = _fox_decode(page_table, col(qf_s), col(kf_s), col(vf_s),
                              lff_s[:, :HEADS].reshape(nb, HEADS, 1, 1), k_t, v_t, floc, l, 8)
        hcol = lambda a: a.reshape(nb, HEADS, HEAD_DIM, 1)
        hrow = lambda a: a.reshape(nb, HEADS, 1, HEAD_DIM)
        hscl = lambda a: a[:, :HEADS].reshape(nb, HEADS, 1, 1)
        o_m_s, c_new, n_new, m_new = _mlstm_decode(
            hcol(qm_s), hcol(km_s), hrow(vm_s), hrow(om_s), hscl(li_s), hscl(lfm_s),
            state_mlstm_C[l], hcol(state_mlstm_n[l]), state_mlstm_m[l].reshape(nb, HEADS, 1, 1), mnw_row[l])
        o_pool_s, buf_new = _pool_decode(pool_t, u_s, wbd, pscale, l, page_table.shape[1] * page)
        mix = lambda a: a.astype(BF16).reshape(1, nb, -1)
        x1s = _out_proj(mix(o_fox_s), mix(o_m_s), mix(o_pool_s), xs, mod_s, l, w_out_b, nb, True)
        if l % 2 == 0:
            xs = _ffn(x1s, mod_s, l, nw2, wfg[j], wfu[j], wfd[j], fw, nb, tf, True, last)
        else:
            xs = _moe_dense(x1s, mod_s, l, nw2, w_router_p[j], weg[j], weu[j], wed[j], fw, nb, tf, True, last)
        st_s.append((kf_s.reshape(nb, 1, HEADS, HEAD_DIM), vf_s.reshape(nb, 1, HEADS, HEAD_DIM),
                     lff_s[:, :HEADS].reshape(nb, 1, HEADS), c_new, n_new.reshape(nb, HEADS, HEAD_DIM),
                     m_new.reshape(nb, HEADS), jnp.transpose(buf_new, (1, 0, 2))))

    outs_p = [jnp.stack(a) for a in zip(*st_p)]
    outs_s = [jnp.stack(a) for a in zip(*st_s)]
    return (xp, xs.reshape(nb, 1, d), *outs_p, *outs_s)
```

```python
import functools

import jax
import jax.numpy as jnp
from jax import lax
from jax.experimental import pallas as pl
from jax.experimental.pallas import tpu as pltpu

F32 = jnp.float32
BF16 = jnp.bfloat16

HEAD_DIM = 64
LANES = 128
HEADS = 6
MIX_W = HEADS * HEAD_DIM
POOL_WINDOWS = (2, 4, 8, 16)
POOL_W = 256
POOL_HALO = 16
POOL_BUF = 15
MLSTM_CHUNK = 128
N_EXPERTS = 8
EPS = 1e-6
NEG_INF = float("-inf")

_MAIN_W = 7 * MIX_W + POOL_W
_GATE_OFF = _MAIN_W
IN_PAD_W = _MAIN_W + 3 * LANES

VMEM_LIMIT = 56 * 1024 * 1024


def _cparams(sem):
    return pltpu.CompilerParams(dimension_semantics=sem, vmem_limit_bytes=VMEM_LIMIT)


def _sigmoid(x):
    return 1.0 / (1.0 + jnp.exp(-x))


def _log_sigmoid(x):
    return jnp.minimum(x, 0.0) - jnp.log(1.0 + jnp.exp(-jnp.abs(x)))


def _split3(x):
    hi = x.astype(BF16).astype(F32)
    r = x - hi
    mid = r.astype(BF16).astype(F32)
    lo = (r - mid).astype(BF16).astype(F32)
    return hi, mid, lo


def _dot(a, b):
    return jnp.dot(a, b, preferred_element_type=F32)


def _dot_nt(a, b):
    return lax.dot_general(a, b, (((1,), (1,)), ((), ())), preferred_element_type=F32)


def _dot_tn(a, b):
    return lax.dot_general(a, b, (((0,), (0,)), ((), ())), preferred_element_type=F32)


def _tri_dot(tri_bf16, x):
    hi, mid, lo = _split3(x)
    return (_dot(tri_bf16, hi.astype(BF16)) + _dot(tri_bf16, mid.astype(BF16))
            + _dot(tri_bf16, lo.astype(BF16)))


def _dot_tri(x, tri_bf16):
    hi, mid, lo = _split3(x)
    return (_dot(hi.astype(BF16), tri_bf16) + _dot(mid.astype(BF16), tri_bf16)
            + _dot(lo.astype(BF16), tri_bf16))


def _lower_tri(n):
    r = lax.broadcasted_iota(jnp.int32, (n, n), 0)
    c = lax.broadcasted_iota(jnp.int32, (n, n), 1)
    return r >= c


def _modulated_rmsnorm(x, nw, sc, sh):
    y = x * lax.rsqrt(jnp.mean(x * x, axis=-1, keepdims=True) + EPS)
    return (y * nw) * (1.0 + sc) + sh


def _ada_kernel(c_ref, w_ref, b_ref, o_ref):
    c = c_ref[...]
    s = c * _sigmoid(c)
    o_ref[...] = _dot(s.astype(BF16), w_ref[...].astype(BF16)) + b_ref[...]


def _ada_modulation(c_all, w_ada, b_ada):
    depth, d, _ = w_ada.shape
    r = c_all.shape[0]
    return pl.pallas_call(
        _ada_kernel,
        out_shape=jax.ShapeDtypeStruct((depth, 6, r, d), F32),
        grid=(depth, 6),
        in_specs=[
            pl.BlockSpec((r, d), lambda l, j: (0, 0)),
            pl.BlockSpec((None, d, d), lambda l, j: (l, 0, j)),
            pl.BlockSpec((None, 1, d), lambda l, j: (l, 0, j)),
        ],
        out_specs=pl.BlockSpec((None, None, r, d), lambda l, j: (l, j, 0, 0)),
        compiler_params=_cparams(("arbitrary", "arbitrary")),
        name="ada_modulation",
    )(c_all, w_ada, b_ada.reshape(depth, 1, 6 * d))


def _pool_mix(u, halo, pos0, wbd, scale):
    t = u.shape[0]
    ext = jnp.concatenate([halo, u], axis=0)
    p2 = ext + pltpu.roll(ext, 1, 0)
    p4 = p2 + pltpu.roll(p2, 2, 0)
    p8 = p4 + pltpu.roll(p4, 4, 0)
    p16 = p8 + pltpu.roll(p8, 8, 0)
    lane = lax.broadcasted_iota(jnp.int32, (t, POOL_W), 1)
    sums = jnp.where(lane < 64, p2[POOL_HALO:], jnp.where(lane < 128, p4[POOL_HALO:],
                     jnp.where(lane < 192, p8[POOL_HALO:], p16[POOL_HALO:])))
    win = jnp.where(lane < 64, 2.0, jnp.where(lane < 128, 4.0, jnp.where(lane < 192, 8.0, 16.0)))
    pos = (pos0 + lax.broadcasted_iota(jnp.int32, (t, POOL_W), 0)).astype(F32)
    cnt = jnp.minimum(pos + 1.0, win)
    d = sums / cnt - u
    return _dot(d.astype(BF16), wbd) * scale


def _in_proj_prompt_kernel(x_ref, nw_ref, sc_ref, sh_ref, w_ref, gb_ref, wbd_ref, ps_ref,
                           qa_ref, ka_ref, va_ref, kf_ref, vf_ref, lff_ref,
                           qm_ref, km_ref, vm_ref, om_ref, li_ref, lfm_ref, op_ref, ul_ref,
                           fcarry, uhalo):
    t = pl.program_id(1)
    tm = x_ref.shape[0]

    @pl.when(t == 0)
    def _():
        fcarry[...] = jnp.zeros_like(fcarry)
        uhalo[...] = jnp.zeros_like(uhalo)

    hb = _modulated_rmsnorm(x_ref[...], nw_ref[...], sc_ref[...], sh_ref[...]).astype(BF16)

    lane = lax.broadcasted_iota(jnp.int32, (tm, LANES), 1)
    zg = _dot(hb, w_ref[:, _GATE_OFF:IN_PAD_W])
    head_lane = lane < HEADS
    lf_fox = jnp.where(head_lane, _log_sigmoid(zg[:, 0:LANES] + gb_ref[0:1, :]), 0.0)
    li = jnp.where(head_lane, zg[:, LANES:2 * LANES] + gb_ref[1:2, :], 0.0)
    lf_m = jnp.where(head_lane, _log_sigmoid(zg[:, 2 * LANES:3 * LANES] + gb_ref[2:3, :]), 0.0)
    lff_ref[...] = lf_fox
    li_ref[...] = li
    lfm_ref[...] = lf_m

    cum = _tri_dot(_lower_tri(tm).astype(BF16), lf_fox) + fcarry[...]
    fcarry[...] = cum[tm - 1:tm, :]

    zq = _dot(hb, w_ref[:, 0:MIX_W]) * (HEAD_DIM ** -0.5)
    zk = _dot(hb, w_ref[:, MIX_W:2 * MIX_W])
    zv = _dot(hb, w_ref[:, 2 * MIX_W:3 * MIX_W])
    kf_ref[...] = zk
    vf_ref[...] = zv
    low = lane < HEAD_DIM
    for h in range(HEADS):
        p, e = divmod(h, 2)
        sl = slice(p * LANES, (p + 1) * LANES)
        bq, bk, bv = zq[:, sl], zk[:, sl], zv[:, sl]
        if e == 1:
            bq, bk, bv = (pltpu.roll(a, HEAD_DIM, 1) for a in (bq, bk, bv))
        hi, mid, lo = _split3(cum[:, h:h + 1])
        augq = jnp.where(lane == 64, hi, jnp.where(lane == 65, mid, jnp.where(lane == 66, lo,
                         jnp.where(lane < 70, 1.0, 0.0))))
        augk = jnp.where(lane < 67, 1.0, jnp.where(lane == 67, -hi, jnp.where(lane == 68, -mid,
                         jnp.where(lane == 69, -lo, 0.0))))
        augv = jnp.where(lane == 64, 1.0, 0.0)
        qa_ref[h] = jnp.where(low, bq, augq).astype(BF16)
        ka_ref[h] = jnp.where(low, bk, augk).astype(BF16)
        va_ref[h] = jnp.where(low, bv, augv).astype(BF16)

    o = 3 * MIX_W
    qm_ref[...] = _dot(hb, w_ref[:, o:o + MIX_W]).astype(BF16)
    km_ref[...] = (_dot(hb, w_ref[:, o + MIX_W:o + 2 * MIX_W]) * (HEAD_DIM ** -0.5)).astype(BF16)
    vm_ref[...] = _dot(hb, w_ref[:, o + 2 * MIX_W:o + 3 * MIX_W]).astype(BF16)
    om_ref[...] = _dot(hb, w_ref[:, o + 3 * MIX_W:o + 4 * MIX_W])

    u = _dot(hb, w_ref[:, 7 * MIX_W:_MAIN_W])
    op_ref[...] = _pool_mix(u, uhalo[...], t * tm, wbd_ref[...], ps_ref[...]).astype(BF16)
    uhalo[...] = u[tm - POOL_HALO:, :]
    ul_ref[...] = u[tm - POOL_HALO:, :]


def _in_proj_prompt(x, mod, l, nw, w_in_p, gate_b, wbd, pscale, tm):
    b, s, d = x.shape
    nt = s // tm
    row = lambda width, dt: jax.ShapeDtypeStruct((b, s, width), dt)
    head = jax.ShapeDtypeStruct((b, HEADS, s, LANES), BF16)
    tok = lambda width: pl.BlockSpec((None, tm, width), lambda i, t: (i, t, 0))
    hspec = pl.BlockSpec((None, HEADS, tm, LANES), lambda i, t: (i, 0, t, 0))
    modspec = lambda j: pl.BlockSpec((None, None, None, 1, d), lambda i, t: (l, j, i, 0, 0))
    return pl.pallas_call(
        _in_proj_prompt_kernel,
        out_shape=(head, head, head, row(MIX_W, F32), row(MIX_W, F32), row(LANES, F32),
                   row(MIX_W, BF16), row(MIX_W, BF16), row(MIX_W, BF16), row(MIX_W, F32),
                   row(LANES, F32), row(LANES, F32), row(POOL_W, BF16),
                   jax.ShapeDtypeStruct((b, POOL_HALO, POOL_W), F32)),
        grid=(b, nt),
        in_specs=[
            tok(d),
            pl.BlockSpec((None, 1, d), lambda i, t: (l, 0, 0)),
            modspec(1), modspec(0),
            pl.BlockSpec((None, d, IN_PAD_W), lambda i, t: (l, 0, 0)),
            pl.BlockSpec((None, 3, LANES), lambda i, t: (l, 0, 0)),
            pl.BlockSpec((None, POOL_W, POOL_W), lambda i, t: (l, 0, 0)),
            pl.BlockSpec((None, 1, POOL_W), lambda i, t: (l, 0, 0)),
        ],
        out_specs=(hspec, hspec, hspec, tok(MIX_W), tok(MIX_W), tok(LANES),
                   tok(MIX_W), tok(MIX_W), tok(MIX_W), tok(MIX_W), tok(LANES), tok(LANES),
                   tok(POOL_W), pl.BlockSpec((None, POOL_HALO, POOL_W), lambda i, t: (i, 0, 0))),
        scratch_shapes=[pltpu.VMEM((1, LANES), F32), pltpu.VMEM((POOL_HALO, POOL_W), F32)],
        compiler_params=_cparams(("arbitrary", "arbitrary")),
        name="in_proj_prompt",
    )(x, nw, mod, mod, w_in_p, gate_b, wbd, pscale)


def _in_proj_sample_kernel(x_ref, nw_ref, sc_ref, sh_ref, w_ref, gb_ref, z_ref, lff_ref, li_ref, lfm_ref):
    hb = _modulated_rmsnorm(x_ref[...], nw_ref[...], sc_ref[...], sh_ref[...]).astype(BF16)
    z_ref[...] = _dot(hb, w_ref[:, 0:_MAIN_W])
    zg = _dot(hb, w_ref[:, _GATE_OFF:IN_PAD_W])
    lff_ref[...] = _log_sigmoid(zg[:, 0:LANES] + gb_ref[0:1, :])
    li_ref[...] = zg[:, LANES:2 * LANES] + gb_ref[1:2, :]
    lfm_ref[...] = _log_sigmoid(zg[:, 2 * LANES:3 * LANES] + gb_ref[2:3, :])


def _in_proj_sample(x, mod, l, nw, w_in_p, gate_b):
    n, d = x.shape
    gate = jax.ShapeDtypeStruct((n, LANES), F32)
    full = lambda *shape: pl.BlockSpec(shape, lambda i: (0,) * len(shape))
    modspec = lambda j: pl.BlockSpec((None, None, n, d), lambda i: (l, j, 0, 0))
    return pl.pallas_call(
        _in_proj_sample_kernel,
        out_shape=(jax.ShapeDtypeStruct((n, _MAIN_W), F32), gate, gate, gate),
        grid=(1,),
        in_specs=[
            full(n, d),
            pl.BlockSpec((None, 1, d), lambda i: (l, 0, 0)),
            modspec(1), modspec(0),
            pl.BlockSpec((None, d, IN_PAD_W), lambda i: (l, 0, 0)),
            pl.BlockSpec((None, 3, LANES), lambda i: (l, 0, 0)),
        ],
        out_specs=(full(n, _MAIN_W), full(n, LANES), full(n, LANES), full(n, LANES)),
        compiler_params=_cparams(("arbitrary",)),
        name="in_proj_sample",
    )(x, nw, mod, mod, w_in_p, gate_b)


def _fox_prompt_kernel(qa_ref, ka_ref, va_ref, o_ref, *, tq):
    qi = pl.program_id(2)
    q = (qa_ref[0], qa_ref[1])

    def block(e, ki, carry, diag):
        m, acc = carry
        k = ka_ref[e, pl.ds(pl.multiple_of(ki * tq, tq), tq), :]
        v = va_ref[e, pl.ds(pl.multiple_of(ki * tq, tq), tq), :]
        s = _dot_nt(q[e], k)
        if diag:
            s = jnp.where(_lower_tri(tq), s, NEG_INF)
        m_new = jnp.maximum(m, jnp.max(s, axis=-1, keepdims=True))
        p = jnp.exp(s - m_new)
        acc = acc * jnp.exp(m - m_new) + _dot(p.astype(BF16), v)
        return m_new, acc

    def body(ki, carry):
        return block(0, ki, carry[0], False), block(1, ki, carry[1], False)

    init = (jnp.full((tq, 1), NEG_INF, F32), jnp.zeros((tq, LANES), F32))
    carry = lax.fori_loop(0, qi, body, (init, init))
    outs = []
    for e in range(2):
        _, acc = block(e, qi, carry[e], True)
        outs.append(acc / acc[:, HEAD_DIM:HEAD_DIM + 1])
    lane = lax.broadcasted_iota(jnp.int32, (tq, LANES), 1)
    o_ref[...] = jnp.where(lane < HEAD_DIM, outs[0], pltpu.roll(outs[1], HEAD_DIM, 1)).astype(o_ref.dtype)


def _fox_prompt(qa, ka, va, tq):
    b, _, s, _ = qa.shape
    return pl.pallas_call(
        functools.partial(_fox_prompt_kernel, tq=tq),
        out_shape=jax.ShapeDtypeStruct((b, s, MIX_W), BF16),
        grid=(b, HEADS // 2, s // tq),
        in_specs=[
            pl.BlockSpec((None, 2, tq, LANES), lambda i, p, t: (i, p, t, 0)),
            pl.BlockSpec((None, 2, s, LANES), lambda i, p, t: (i, p, 0, 0)),
            pl.BlockSpec((None, 2, s, LANES), lambda i, p, t: (i, p, 0, 0)),
        ],
        out_specs=pl.BlockSpec((None, tq, LANES), lambda i, p, t: (i, t, p)),
        compiler_params=_cparams(("arbitrary", "arbitrary", "arbitrary")),
        name="fox_prompt",
    )(qa, ka, va)


def _mlstm_prompt_kernel(q_ref, k_ref, v_ref, og_ref, li_ref, lf_ref, nw_ref,
                         o_ref, c_out, n_out, m_out, c_st, n_st, m_st):
    c = pl.program_id(1)
    nb, L, _ = q_ref.shape

    @pl.when(c == 0)
    def _():
        c_st[...] = jnp.zeros_like(c_st)
        n_st[...] = jnp.zeros_like(n_st)
        m_st[...] = jnp.zeros_like(m_st)

    tril = _lower_tri(L)
    tri = tril.astype(BF16)
    lane = lax.broadcasted_iota(jnp.int32, (L, LANES), 1)
    lane1 = lax.broadcasted_iota(jnp.int32, (1, LANES), 1)
    rowi = lax.broadcasted_iota(jnp.int32, (LANES, LANES), 0)
    coli = lax.broadcasted_iota(jnp.int32, (LANES, LANES), 1)
    same_head = (rowi < HEAD_DIM) == (coli < HEAD_DIM)

    for bi in range(nb):
        lf = lf_ref[bi]
        li = li_ref[bi]
        bcum = _tri_dot(tri, lf)
        m_prev = m_st[bi]
        b_end = bcum[L - 1:L, :]
        g = b_end - bcum + li
        m_new = jnp.maximum(b_end + m_prev, jnp.max(g, axis=0, keepdims=True))
        a_state = jnp.exp(b_end + m_prev - m_new)
        wg = jnp.exp(g - m_new)
        inter = bcum + m_prev
        bcum_t = bcum.T
        li_t = li.T
        for p in range(HEADS // 2):
            sl = slice(p * LANES, (p + 1) * LANES)
            q = q_ref[bi, :, sl]
            k = k_ref[bi, :, sl]
            v = v_ref[bi, :, sl]
            cp = c_st[bi, p]
            npair = n_st[bi, p]
            qc = _dot(q, cp.astype(BF16))
            qn_full = q.astype(F32) * npair
            hpair = jnp.zeros((L, LANES), F32)
            for e in range(2):
                h = 2 * p + e
                mine = (lane < HEAD_DIM) if e == 0 else (lane >= HEAD_DIM)
                dmat = jnp.where(tril, bcum[:, h:h + 1] - bcum_t[h:h + 1, :] + li_t[h:h + 1, :], NEG_INF)
                int_h = inter[:, h:h + 1]
                m_t = jnp.maximum(int_h, jnp.max(dmat, axis=-1, keepdims=True))
                a_int = jnp.exp(int_h - m_t)
                s = _dot_nt(jnp.where(mine, q, jnp.zeros_like(q)), k) * jnp.exp(dmat - m_t)
                sv = _dot(s.astype(BF16), v)
                den = a_int * jnp.sum(jnp.where(mine, qn_full, 0.0), axis=-1, keepdims=True) \
                    + jnp.sum(s, axis=-1, keepdims=True)
                hh = (a_int * qc + sv) / jnp.maximum(jnp.abs(den), jnp.exp(-m_t))
                ms = jnp.sum(jnp.where(mine, hh * hh, 0.0), axis=-1, keepdims=True) * (1.0 / HEAD_DIM)
                hpair = jnp.where(mine, hh * lax.rsqrt(ms + EPS), hpair)
            o_ref[bi, :, sl] = (_sigmoid(og_ref[bi, :, sl]) * (hpair * nw_ref[:, sl])).astype(o_ref.dtype)
            wgp = jnp.where(lane < HEAD_DIM, wg[:, 2 * p:2 * p + 1], wg[:, 2 * p + 1:2 * p + 2])
            ap = jnp.where(lane1 < HEAD_DIM, a_state[:, 2 * p:2 * p + 1], a_state[:, 2 * p + 1:2 * p + 2])
            kv = _dot_tn(k, (v.astype(F32) * wgp).astype(BF16))
            c_st[bi, p] = cp * ap + jnp.where(same_head, kv, 0.0)
            n_st[bi, p] = npair * ap + jnp.sum(k.astype(F32) * wgp, axis=0, keepdims=True)
        m_st[bi] = m_new

    @pl.when(c == pl.num_programs(1) - 1)
    def _():
        c_out[...] = c_st[...]
        n_out[...] = n_st[...]
        m_out[...] = m_st[...]


def _mlstm_prompt(qm, km, vm, om, li, lfm, nw, l, nb):
    b, s, _ = qm.shape
    L = MLSTM_CHUNK
    np_ = HEADS // 2
    tok = lambda w: pl.BlockSpec((nb, L, w), lambda i, c: (i, c, 0))
    return pl.pallas_call(
        _mlstm_prompt_kernel,
        out_shape=(jax.ShapeDtypeStruct((b, s, MIX_W), BF16),
                   jax.ShapeDtypeStruct((b, np_, LANES, LANES), F32),
                   jax.ShapeDtypeStruct((b, np_, 1, LANES), F32),
                   jax.ShapeDtypeStruct((b, 1, LANES), F32)),
        grid=(b // nb, s // L),
        in_specs=[tok(MIX_W), tok(MIX_W), tok(MIX_W), tok(MIX_W), tok(LANES), tok(LANES),
                  pl.BlockSpec((None, 1, MIX_W), lambda i, c: (l, 0, 0))],
        out_specs=(tok(MIX_W),
                   pl.BlockSpec((nb, np_, LANES, LANES), lambda i, c: (i, 0, 0, 0)),
                   pl.BlockSpec((nb, np_, 1, LANES), lambda i, c: (i, 0, 0, 0)),
                   pl.BlockSpec((nb, 1, LANES), lambda i, c: (i, 0, 0))),
        scratch_shapes=[pltpu.VMEM((nb, np_, LANES, LANES), F32),
                        pltpu.VMEM((nb, np_, 1, LANES), F32),
                        pltpu.VMEM((nb, 1, LANES), F32)],
        compiler_params=_cparams(("arbitrary", "arbitrary")),
        name="mlstm_prompt",
    )(qm, km, vm, om, li, lfm, nw)


def _out_proj_kernel(of_ref, om_ref, op_ref, x_ref, g_ref, w_ref, o_ref):
    y = (_dot(of_ref[...], w_ref[0:MIX_W, :]) + _dot(om_ref[...], w_ref[MIX_W:2 * MIX_W, :])
         + _dot(op_ref[...], w_ref[2 * MIX_W:, :]))
    o_ref[...] = x_ref[...] + g_ref[...] * y


def _out_proj(of, om, op, x, mod, l, w_out_b, tm, per_row_mod):
    b, s, d = x.shape
    tok = lambda w: pl.BlockSpec((None, tm, w), lambda i, t: (i, t, 0))
    if per_row_mod:
        gspec = pl.BlockSpec((None, None, tm, d), lambda i, t: (l, 2, 0, 0))
    else:
        gspec = pl.BlockSpec((None, None, None, 1, d), lambda i, t: (l, 2, i, 0, 0))
    return pl.pallas_call(
        _out_proj_kernel,
        out_shape=jax.ShapeDtypeStruct((b, s, d), F32),
        grid=(b, s // tm),
        in_specs=[tok(MIX_W), tok(MIX_W), tok(POOL_W), tok(d), gspec,
                  pl.BlockSpec((None, d, d), lambda i, t: (l, 0, 0))],
        out_specs=tok(d),
        compiler_params=_cparams(("arbitrary", "arbitrary")),
        name="out_proj",
    )(of, om, op, x, mod, w_out_b)


def _final_norm(x, fw):
    return (x * lax.rsqrt(jnp.mean(x * x, axis=-1, keepdims=True) + EPS)) * fw


def _ffn_kernel(x_ref, nw_ref, sc_ref, sh_ref, g_ref, wg_ref, wu_ref, wd_ref, fw_ref, o_ref,
                h_sc, acc, *, final):
    j = pl.program_id(2)

    @pl.when(j == 0)
    def _():
        h_sc[...] = _modulated_rmsnorm(x_ref[...], nw_ref[...], sc_ref[...], sh_ref[...]).astype(BF16)
        acc[...] = jnp.zeros_like(acc)

    h = h_sc[...]
    a = _dot(h, wg_ref[...])
    u = _dot(h, wu_ref[...])
    acc[...] += _dot(((a * _sigmoid(a)) * u).astype(BF16), wd_ref[...])

    @pl.when(j == pl.num_programs(2) - 1)
    def _():
        y = x_ref[...] + g_ref[...] * acc[...]
        o_ref[...] = _final_norm(y, fw_ref[...]) if final else y


def _mod_specs(l, d, tm, per_row_mod, nidx):
    def spec(j):
        if per_row_mod:
            return pl.BlockSpec((None, None, tm, d), lambda *a: (l, j, 0, 0))
        return pl.BlockSpec((None, None, None, 1, d), lambda *a: (l, j, a[0], 0, 0))
    return spec(4), spec(3), spec(5)


def _ffn(x, mod, l, nw, wg, wu, wd, fw, tm, tf, per_row_mod, final):
    b, s, d = x.shape
    ff = wg.shape[-1]
    sc, sh, g = _mod_specs(l, d, tm, per_row_mod, 3)
    return pl.pallas_call(
        functools.partial(_ffn_kernel, final=final),
        out_shape=jax.ShapeDtypeStruct((b, s, d), F32),
        grid=(b, s // tm, ff // tf),
        in_specs=[
            pl.BlockSpec((None, tm, d), lambda i, t, j: (i, t, 0)),
            pl.BlockSpec((None, 1, d), lambda i, t, j: (l, 0, 0)),
            sc, sh, g,
            pl.BlockSpec((d, tf), lambda i, t, j: (0, j)),
            pl.BlockSpec((d, tf), lambda i, t, j: (0, j)),
            pl.BlockSpec((tf, d), lambda i, t, j: (j, 0)),
            pl.BlockSpec((1, d), lambda i, t, j: (0, 0)),
        ],
        out_specs=pl.BlockSpec((None, tm, d), lambda i, t, j: (i, t, 0)),
        scratch_shapes=[pltpu.VMEM((tm, d), BF16), pltpu.VMEM((tm, d), F32)],
        compiler_params=_cparams(("arbitrary", "arbitrary", "arbitrary")),
        name="ffn_swiglu",
    )(x, nw, mod, mod, mod, wg, wu, wd, fw)


def _top2(logits):
    lane = lax.broadcasted_iota(jnp.int32, logits.shape, 1)
    valid = lane < N_EXPERTS
    z = jnp.where(valid, logits, NEG_INF)
    pz = jnp.exp(z - jnp.max(z, axis=-1, keepdims=True))
    probs = pz / jnp.sum(pz, axis=-1, keepdims=True)
    m1 = jnp.max(probs, axis=-1, keepdims=True)
    i1 = jnp.min(jnp.where(probs == m1, lane, LANES), axis=-1, keepdims=True)
    rest = jnp.where((lane == i1) | (~valid), -1.0, probs)
    m2 = jnp.max(rest, axis=-1, keepdims=True)
    i2 = jnp.min(jnp.where(rest == m2, lane, LANES), axis=-1, keepdims=True)
    tot = m1 + m2
    return i1, i2, m1 / tot, m2 / tot


_R_E1, _R_E2, _R_R1, _R_R2, _R_G1, _R_G2 = range(6)


def _moe_route_kernel(x_ref, nw_ref, sc_ref, sh_ref, wr_ref, h_ref, route_ref, cnt_ref, carry):
    first = (pl.program_id(0) == 0) & (pl.program_id(1) == 0)

    @pl.when(first)
    def _():
        carry[...] = jnp.zeros_like(carry)

    h = _modulated_rmsnorm(x_ref[...], nw_ref[...], sc_ref[...], sh_ref[...])
    h_ref[...] = h
    tm = h.shape[0]
    i1, i2, g1, g2 = _top2(_dot(h.astype(BF16), wr_ref[...]))
    lane = lax.broadcasted_iota(jnp.int32, (tm, LANES), 1)
    oh1 = lane == i1
    oh2 = lane == i2
    both = jnp.where(oh1 | oh2, 1.0, 0.0)
    r = lax.broadcasted_iota(jnp.int32, (tm, tm), 0)
    c = lax.broadcasted_iota(jnp.int32, (tm, tm), 1)
    before = carry[...] + _dot((r > c).astype(BF16), both.astype(BF16))
    r1 = jnp.sum(jnp.where(oh1, before, 0.0), axis=-1, keepdims=True)
    r2 = jnp.sum(jnp.where(oh2, before, 0.0), axis=-1, keepdims=True)
    carry[...] += jnp.sum(both, axis=0, keepdims=True)
    cnt_ref[...] = carry[...]
    vals = (i1.astype(F32), i2.astype(F32), r1, r2, g1, g2)
    route = jnp.zeros((tm, LANES), F32)
    for k, v in enumerate(vals):
        route = jnp.where(lane == k, v, route)
    route_ref[...] = route


def _moe_route(x, mod, l, nw, wr, tm, per_row_mod):
    b, s, d = x.shape
    sc, sh, _ = _mod_specs(l, d, tm, per_row_mod, 2)
    tok = lambda w: pl.BlockSpec((None, tm, w), lambda i, t: (i, t, 0))
    return pl.pallas_call(
        _moe_route_kernel,
        out_shape=(jax.ShapeDtypeStruct((b, s, d), F32), jax.ShapeDtypeStruct((b, s, LANES), F32),
                   jax.ShapeDtypeStruct((1, LANES), F32)),
        grid=(b, s // tm),
        in_specs=[tok(d), pl.BlockSpec((None, 1, d), lambda i, t: (l, 0, 0)), sc, sh,
                  pl.BlockSpec((d, LANES), lambda i, t: (0, 0))],
        out_specs=(tok(d), tok(LANES), pl.BlockSpec((1, LANES), lambda i, t: (0, 0))),
        scratch_shapes=[pltpu.VMEM((1, LANES), F32)],
        compiler_params=_cparams(("arbitrary", "arbitrary")),
        name="moe_route",
    )(x, nw, mod, mod, wr)


def _row_copy(src, src_row, dst, dst_row, sem):
    return pltpu.make_async_copy(src.at[pl.ds(src_row, 1)], dst.at[pl.ds(dst_row, 1)], sem)


def _moe_dispatch_kernel(d1_ref, d2_ref, h_ref, hs_in, hs_ref, sem):
    del hs_in
    tm = h_ref.shape[0]

    def start(r, _):
        _row_copy(h_ref, r, hs_ref, d1_ref[0, r], sem).start()
        _row_copy(h_ref, r, hs_ref, d2_ref[0, r], sem).start()
        return 0

    def wait(r, _):
        _row_copy(h_ref, r, hs_ref, d1_ref[0, r], sem).wait()
        _row_copy(h_ref, r, hs_ref, d2_ref[0, r], sem).wait()
        return 0

    lax.fori_loop(0, tm, start, 0, unroll=8)
    lax.fori_loop(0, tm, wait, 0, unroll=8)


def _moe_dispatch(h, dest1, dest2, rows, tm):
    n, d = h.shape
    idx = pl.BlockSpec((None, 1, tm), lambda i: (i, 0, 0), memory_space=pltpu.SMEM)
    return pl.pallas_call(
        _moe_dispatch_kernel,
        out_shape=jax.ShapeDtypeStruct((rows, d), F32),
        grid=(n // tm,),
        in_specs=[idx, idx, pl.BlockSpec((tm, d), lambda i: (i, 0)), pl.BlockSpec(memory_space=pl.ANY)],
        out_specs=pl.BlockSpec(memory_space=pl.ANY),
        scratch_shapes=[pltpu.SemaphoreType.DMA],
        input_output_aliases={3: 0},
        compiler_params=_cparams(("arbitrary",)),
        name="moe_dispatch",
    )(dest1, dest2, h, jnp.zeros((rows, d), F32))


def _moe_experts_kernel(te_ref, nv_ref, x_ref, wg_ref, wu_ref, wd_ref, o_ref, *, halves):
    del te_ref
    i = pl.program_id(0)

    @pl.when(i < nv_ref[0])
    def _():
        x = x_ref[...].astype(BF16)
        ff = wg_ref.shape[-1]
        hw = ff // halves
        y = None
        for k in range(halves):
            sl = slice(k * hw, (k + 1) * hw)
            a = _dot(x, wg_ref[:, sl])
            u = _dot(x, wu_ref[:, sl])
            part = _dot(((a * _sigmoid(a)) * u).astype(BF16), wd_ref[sl, :])
            y = part if y is None else y + part
        o_ref[...] = y

    @pl.when(i >= nv_ref[0])
    def _():
        o_ref[...] = jnp.zeros_like(o_ref)


def _moe_experts(hs, tile_expert, n_valid, wg, wu, wd, tg):
    rows, d = hs.shape
    ff = wg.shape[-1]
    grid_spec = pltpu.PrefetchScalarGridSpec(
        num_scalar_prefetch=2,
        grid=(rows // tg,),
        in_specs=[pl.BlockSpec((tg, d), lambda i, te, nv: (i, 0)),
                  pl.BlockSpec((None, d, ff), lambda i, te, nv: (te[i], 0, 0)),
                  pl.BlockSpec((None, d, ff), lambda i, te, nv: (te[i], 0, 0)),
                  pl.BlockSpec((None, ff, d), lambda i, te, nv: (te[i], 0, 0))],
        out_specs=pl.BlockSpec((tg, d), lambda i, te, nv: (i, 0)),
    )
    return pl.pallas_call(
        functools.partial(_moe_experts_kernel, halves=2),
        out_shape=jax.ShapeDtypeStruct((rows, d), F32),
        grid_spec=grid_spec,
        compiler_params=_cparams(("arbitrary",)),
        name="moe_experts",
    )(tile_expert, n_valid, hs, wg, wu, wd)


def _moe_combine_kernel(d1_ref, d2_ref, x_ref, g_ref, route_ref, fw_ref, ys_ref, o_ref, buf1, buf2, sem,
                        *, final):
    tm = x_ref.shape[0]

    def start(r, _):
        _row_copy(ys_ref, d1_ref[0, r], buf1, r, sem).start()
        _row_copy(ys_ref, d2_ref[0, r], buf2, r, sem).start()
        return 0

    def wait(r, _):
        _row_copy(ys_ref, d1_ref[0, r], buf1, r, sem).wait()
        _row_copy(ys_ref, d2_ref[0, r], buf2, r, sem).wait()
        return 0

    lax.fori_loop(0, tm, start, 0, unroll=8)
    lax.fori_loop(0, tm, wait, 0, unroll=8)
    route = route_ref[...]
    f = route[:, _R_G1:_R_G1 + 1] * buf1[...] + route[:, _R_G2:_R_G2 + 1] * buf2[...]
    y = x_ref[...] + g_ref[...] * f
    o_ref[...] = _final_norm(y, fw_ref[...]) if final else y


def _moe_combine(x, mod, l, route, dest1, dest2, ys, fw, tm, per_row_mod, final):
    b, s, d = x.shape
    nt = s // tm
    _, _, g = _mod_specs(l, d, tm, per_row_mod, 2)
    idx = pl.BlockSpec((None, 1, tm), lambda i, t: (i * nt + t, 0, 0), memory_space=pltpu.SMEM)
    tok = lambda w: pl.BlockSpec((None, tm, w), lambda i, t: (i, t, 0))
    return pl.pallas_call(
        functools.partial(_moe_combine_kernel, final=final),
        out_shape=jax.ShapeDtypeStruct((b, s, d), F32),
        grid=(b, nt),
        in_specs=[idx, idx, tok(d), g, tok(LANES), pl.BlockSpec((1, d), lambda i, t: (0, 0)),
                  pl.BlockSpec(memory_space=pl.ANY)],
        out_specs=tok(d),
        scratch_shapes=[pltpu.VMEM((tm, d), F32), pltpu.VMEM((tm, d), F32), pltpu.SemaphoreType.DMA],
        compiler_params=_cparams(("arbitrary", "arbitrary")),
        name="moe_combine",
    )(dest1, dest2, x, mod, route, fw, ys)


def _moe(x, mod, l, nw, wr, wg, wu, wd, fw, tm, tg, per_row_mod, final):
    b, s, d = x.shape
    n = b * s
    h, route, counts = _moe_route(x, mod, l, nw, wr, tm, per_row_mod)
    cnt = counts[0, :N_EXPERTS].astype(jnp.int32)
    padded = (cnt + tg - 1) // tg * tg
    ends = jnp.cumsum(padded)
    offs = ends - padded
    rows = (2 * n + N_EXPERTS * (tg - 1) + tg - 1) // tg * tg
    n_valid = ends[-1] // tg
    tile = jnp.minimum(jnp.arange(rows // tg, dtype=jnp.int32), n_valid - 1)
    tile_expert = jnp.minimum(jnp.sum(tile[:, None] * tg >= ends[None, :], axis=1), N_EXPERTS - 1).astype(jnp.int32)
    rt = route.reshape(n, LANES)
    dest = lambda ke, kr: (offs[rt[:, ke].astype(jnp.int32)] + rt[:, kr].astype(jnp.int32)).reshape(n // tm, 1, tm)
    dest1, dest2 = dest(_R_E1, _R_R1), dest(_R_E2, _R_R2)
    hs = _moe_dispatch(h.reshape(n, d), dest1, dest2, rows, tm)
    ys = _moe_experts(hs, tile_expert, n_valid.reshape(1), wg, wu, wd, tg)
    return _moe_combine(x, mod, l, route, dest1, dest2, ys, fw, tm, per_row_mod, final)


def _page_cumsum_kernel(lf_ref, o_ref):
    n = lf_ref.shape[-1]
    r = lax.broadcasted_iota(jnp.int32, (n, n), 0)
    c = lax.broadcasted_iota(jnp.int32, (n, n), 1)
    o_ref[...] = _dot_tri(lf_ref[...], (r <= c).astype(BF16))


def _page_cumsum(lf_t):
    depth, h, n_pool, page = lf_t.shape
    return pl.pallas_call(
        _page_cumsum_kernel,
        out_shape=jax.ShapeDtypeStruct(lf_t.shape, F32),
        grid=(depth, h),
        in_specs=[pl.BlockSpec((None, None, n_pool, page), lambda l, i: (l, i, 0, 0))],
        out_specs=pl.BlockSpec((None, None, n_pool, page), lambda l, i: (l, i, 0, 0)),
        compiler_params=_cparams(("arbitrary", "arbitrary")),
        name="page_cumsum",
    )(lf_t)


def _fox_decode_kernel(pt_ref, q_ref, kn_ref, vn_ref, lfn_ref, k_hbm, v_hbm, f_hbm, o_ref,
                       pbuf, fbuf, psem, fsem, s_sc, acc, *, l, nbuf):
    b = pl.program_id(0)
    n_pages = pt_ref.shape[1]
    page = pbuf.shape[-1]
    per_seq = 2 * n_pages
    total = pl.num_programs(0) * per_seq

    def page_copy(src, seq, j, slot):
        return pltpu.make_async_copy(src.at[l, pt_ref[seq, j]], pbuf.at[slot], psem.at[slot])

    def f_copy(seq, j, slot):
        pg = pt_ref[seq, j]
        src = f_hbm.at[l, :, lax.shift_right_logical(pg, 3), pl.ds(pg & 7, 1), :]
        return pltpu.make_async_copy(src, fbuf.at[slot], fsem.at[slot])

    def start_item(g, slot):
        seq = lax.div(g, per_seq)
        i = g - seq * per_seq

        @pl.when(i < n_pages)
        def _():
            page_copy(k_hbm, seq, i, slot).start()
            f_copy(seq, i, slot).start()

        @pl.when(i >= n_pages)
        def _():
            page_copy(v_hbm, seq, i - n_pages, slot).start()

    @pl.when(b == 0)
    def _():
        for s in range(nbuf):
            page_copy(k_hbm, 0, s, s).start()
            f_copy(0, s, s).start()

    q3 = q_ref[...].reshape(HEADS, HEAD_DIM, 1) * (HEAD_DIM ** -0.5)

    def k_body(i, run):
        slot = i & (nbuf - 1)
        page_copy(k_hbm, b, i, slot).wait()
        f_copy(b, i, slot).wait()
        floc = fbuf[slot]
        sc = jnp.sum(pbuf[slot] * q3, axis=1, keepdims=True)
        s_sc[:, :, pl.ds(pl.multiple_of(i * page, page), page)] = sc - (run + floc)
        nxt = b * per_seq + i + nbuf

        @pl.when(nxt < total)
        def _():
            start_item(nxt, slot)

        return run + floc[:, :, page - 1:page]

    run = lax.fori_loop(0, n_pages, k_body, jnp.zeros((HEADS, 1, 1), F32))

    s_all = s_sc[...]
    s_new = jnp.sum(q3 * kn_ref[...].reshape(HEADS, HEAD_DIM, 1), axis=1, keepdims=True) - (run + lfn_ref[...])
    m = jnp.maximum(jnp.max(s_all, axis=2, keepdims=True), s_new)
    p = jnp.exp(s_all - m)
    p_new = jnp.exp(s_new - m)
    s_sc[...] = p
    denom = jnp.sum(p, axis=2, keepdims=True) + p_new
    acc[...] = jnp.zeros_like(acc)

    def v_body(i, _):
        slot = i & (nbuf - 1)
        page_copy(v_hbm, b, i, slot).wait()
        acc[...] += pbuf[slot] * s_sc[:, :, pl.ds(pl.multiple_of(i * page, page), page)]
        nxt = b * per_seq + n_pages + i + nbuf

        @pl.when(nxt < total)
        def _():
            start_item(nxt, slot)

        return 0

    lax.fori_loop(0, n_pages, v_body, 0)
    o3 = jnp.sum(acc[...], axis=2, keepdims=True) + p_new * vn_ref[...].reshape(HEADS, HEAD_DIM, 1)
    o_ref[...] = (o3 / denom).reshape(MIX_W, 1)


def _fox_decode(page_table, q_col, kn_col, vn_col, lfn, k_t, v_t, floc, l, nbuf):
    nb, n_pages = page_table.shape
    page = k_t.shape[-1]
    assert nbuf <= n_pages and nbuf & (nbuf - 1) == 0 and n_pages % nbuf == 0
    col = pl.BlockSpec((None, MIX_W, 1), lambda i, pt: (i, 0, 0))
    hbm = pl.BlockSpec(memory_space=pl.ANY)
    grid_spec = pltpu.PrefetchScalarGridSpec(
        num_scalar_prefetch=1,
        grid=(nb,),
        in_specs=[col, col, col, pl.BlockSpec((None, HEADS, 1, 1), lambda i, pt: (i, 0, 0, 0)), hbm, hbm, hbm],
        out_specs=col,
        scratch_shapes=[pltpu.VMEM((nbuf, HEADS, HEAD_DIM, page), F32),
                        pltpu.VMEM((nbuf, HEADS, 1, page), F32),
                        pltpu.SemaphoreType.DMA((nbuf,)),
                        pltpu.SemaphoreType.DMA((nbuf,)),
                        pltpu.VMEM((HEADS, 1, n_pages * page), F32),
                        pltpu.VMEM((HEADS, HEAD_DIM, page), F32)],
    )
    return pl.pallas_call(
        functools.partial(_fox_decode_kernel, l=l, nbuf=nbuf),
        out_shape=jax.ShapeDtypeStruct((nb, MIX_W, 1), F32),
        grid_spec=grid_spec,
        compiler_params=_cparams(("arbitrary",)),
        name="fox_decode",
    )(page_table, q_col, kn_col, vn_col, lfn, k_t, v_t, floc)


def _mlstm_decode_kernel(q_ref, k_ref, v_ref, og_ref, li_ref, lf_ref, c_ref, n_ref, m_ref, nw_ref,
                         o_ref, c_out, n_out, m_out):
    q = q_ref[...]
    k = k_ref[...] * (HEAD_DIM ** -0.5)
    v = v_ref[...]
    c = c_ref[...]
    n = n_ref[...]
    m = m_ref[...]
    li = li_ref[...]
    lf = lf_ref[...]
    inter = lf + m
    m_t = jnp.maximum(inter, li)
    a_int = jnp.exp(inter - m_t)
    w_in = jnp.exp(li - m_t)
    s = jnp.sum(q * k, axis=1, keepdims=True) * w_in
    num = a_int * jnp.sum(q * c, axis=1, keepdims=True) + s * v
    den = a_int * jnp.sum(q * n, axis=1, keepdims=True) + s
    h = num / jnp.maximum(jnp.abs(den), jnp.exp(-m_t))
    h = h * lax.rsqrt(jnp.mean(h * h, axis=-1, keepdims=True) + EPS)
    o_ref[...] = _sigmoid(og_ref[...]) * (h * nw_ref[...])
    c_out[...] = a_int * c + w_in * (k * v)
    n_out[...] = a_int * n + w_in * k
    m_out[...] = m_t


def _mlstm_decode(q_col, k_col, v_row, og_row, li, lf, c0, n0_col, m0, nw_row):
    nb = q_col.shape[0]
    d = HEAD_DIM
    colspec = pl.BlockSpec((None, HEADS, d, 1), lambda i: (i, 0, 0, 0))
    rowspec = pl.BlockSpec((None, HEADS, 1, d), lambda i: (i, 0, 0, 0))
    sclspec = pl.BlockSpec((None, HEADS, 1, 1), lambda i: (i, 0, 0, 0))
    matspec = pl.BlockSpec((None, HEADS, d, d), lambda i: (i, 0, 0, 0))
    return pl.pallas_call(
        _mlstm_decode_kernel,
        out_shape=(jax.ShapeDtypeStruct((nb, HEADS, 1, d), F32),
                   jax.ShapeDtypeStruct((nb, HEADS, d, d), F32),
                   jax.ShapeDtypeStruct((nb, HEADS, d, 1), F32),
                   jax.ShapeDtypeStruct((nb, HEADS, 1, 1), F32)),
        grid=(nb,),
        in_specs=[colspec, colspec, rowspec, rowspec, sclspec, sclspec, matspec, colspec, sclspec,
                  pl.BlockSpec((HEADS, 1, d), lambda i: (0, 0, 0))],
        out_specs=(rowspec, matspec, colspec, sclspec),
        compiler_params=_cparams(("arbitrary",)),
        name="mlstm_decode",
    )(q_col, k_col, v_row, og_row, li, lf, c0, n0_col, m0, nw_row)


def _pool_decode_kernel(buf_ref, u_ref, wbd_ref, ps_ref, o_ref, nb_ref, *, pos):
    u = u_ref[...]
    n = buf_ref.shape[0]
    lane = lax.broadcasted_iota(jnp.int32, u.shape, 1)
    sums = {}
    run = u
    for back in range(1, max(POOL_WINDOWS)):
        run = run + buf_ref[n - back]
        if back + 1 in POOL_WINDOWS:
            sums[back + 1] = run
    cnt = {w: float(min(pos + 1, w)) for w in POOL_WINDOWS}
    mean = jnp.where(lane < 64, sums[2] / cnt[2], jnp.where(lane < 128, sums[4] / cnt[4],
                     jnp.where(lane < 192, sums[8] / cnt[8], sums[16] / cnt[16])))
    o_ref[...] = _dot((mean - u).astype(BF16), wbd_ref[...]) * ps_ref[...]
    for r in range(n - 1):
        nb_ref[r] = buf_ref[r + 1]
    nb_ref[n - 1] = u


def _pool_decode(buf_t, u, wbd, pscale, l, pos):
    _, n, nb, w = buf_t.shape
    return pl.pallas_call(
        functools.partial(_pool_decode_kernel, pos=pos),
        out_shape=(jax.ShapeDtypeStruct((nb, w), F32), jax.ShapeDtypeStruct((n, nb, w), F32)),
        grid=(1,),
        in_specs=[pl.BlockSpec((None, n, nb, w), lambda i: (l, 0, 0, 0)),
                  pl.BlockSpec((nb, w), lambda i: (0, 0)),
                  pl.BlockSpec((None, w, w), lambda i: (l, 0, 0)),
                  pl.BlockSpec((None, 1, w), lambda i: (l, 0, 0))],
        out_specs=(pl.BlockSpec((nb, w), lambda i: (0, 0)),
                   pl.BlockSpec((n, nb, w), lambda i: (0, 0, 0))),
        compiler_params=_cparams(("arbitrary",)),
        name="pool_decode",
    )(buf_t, u, wbd, pscale)


def _tile(n, pref):
    t = min(n, pref)
    assert n % t == 0, (n, t)
    return t


def kernel(x_prompt, x_sample, c_prompt, c_sample, cache_fox_k, cache_fox_v, cache_fox_lf, page_table,
           state_mlstm_C, state_mlstm_n, state_mlstm_m, state_pool, w_ada, b_ada, norm1_w, norm2_w,
           w_in, b_fox_f, b_mlstm_i, b_mlstm_f, mlstm_norm_w, w_pool, pool_scale, w_out,
           w_ffn_gate, w_ffn_up, w_ffn_down, w_router, w_exp_gate, w_exp_up, w_exp_down, final_norm_w):
    depth = w_in.shape[0]
    b, s, d = x_prompt.shape
    nb = x_sample.shape[0]
    n_pool, page = cache_fox_k.shape[1], cache_fox_k.shape[2]
    assert x_sample.shape[1] == 1 and s % MLSTM_CHUNK == 0 and n_pool % 8 == 0

    pad_gate = lambda w: jnp.pad(w, ((0, 0), (0, 0), (0, LANES - HEADS)))
    w_in_p = jnp.concatenate(
        [w_in[:, :, :_MAIN_W], pad_gate(w_in[:, :, _MAIN_W:_MAIN_W + HEADS]),
         pad_gate(w_in[:, :, _MAIN_W + HEADS:_MAIN_W + 2 * HEADS]),
         pad_gate(w_in[:, :, _MAIN_W + 2 * HEADS:])], axis=-1).astype(BF16)
    gate_b = jnp.pad(jnp.stack([b_fox_f, b_mlstm_i, b_mlstm_f], axis=1), ((0, 0), (0, 0), (0, LANES - HEADS)))
    wbd = jnp.zeros((depth, POOL_W, POOL_W), F32)
    for g in range(len(POOL_WINDOWS)):
        sl = slice(g * HEAD_DIM, (g + 1) * HEAD_DIM)
        wbd = wbd.at[:, sl, sl].set(w_pool[:, g])
    wbd = wbd.astype(BF16)
    pscale = pool_scale.reshape(depth, 1, POOL_W)
    w_out_b = w_out.astype(BF16)
    nw1 = norm1_w.reshape(depth, 1, d)
    nw2 = norm2_w.reshape(depth, 1, d)
    mnw = mlstm_norm_w.reshape(depth, 1, MIX_W)
    mnw_row = mlstm_norm_w.reshape(depth, HEADS, 1, HEAD_DIM)
    fw = final_norm_w.reshape(1, d)
    w_router_p = jnp.pad(w_router, ((0, 0), (0, 0), (0, LANES - N_EXPERTS))).astype(BF16)
    wfg, wfu, wfd = (w.astype(BF16) for w in (w_ffn_gate, w_ffn_up, w_ffn_down))
    weg, weu, wed = (w.astype(BF16) for w in (w_exp_gate, w_exp_up, w_exp_down))

    k_t = jnp.transpose(cache_fox_k, (0, 1, 3, 4, 2))
    v_t = jnp.transpose(cache_fox_v, (0, 1, 3, 4, 2))
    lf_t = jnp.transpose(cache_fox_lf, (0, 3, 1, 2))
    floc = _page_cumsum(lf_t).reshape(depth, HEADS, n_pool // 8, 8, page)
    pool_t = jnp.transpose(state_pool, (0, 2, 1, 3))

    mod = _ada_modulation(jnp.concatenate([c_prompt, c_sample], axis=0), w_ada, b_ada)
    mod_p = mod[:, :, :b].reshape(depth, 6, b, 1, d)
    mod_s = mod[:, :, b:]

    tm = _tile(s, 512)
    tq = _tile(s, 512)
    tf = w_ffn_gate.shape[-1] // 2
    xp = x_prompt
    xs = x_sample.reshape(1, nb, d)
    st_p, st_s = [], []
    for l in range(depth):
        last = l == depth - 1
        (qa, ka, va, kf, vf, lff, qm, km, vm, om, li, lfm, opool, ulast) = _in_proj_prompt(
            xp, mod_p, l, nw1, w_in_p, gate_b, wbd, pscale, tm)
        o_fox = _fox_prompt(qa, ka, va, tq)
        o_m, c_pair, n_pair, m_fin = _mlstm_prompt(qm, km, vm, om, li, lfm, mnw, l, 2 if b % 2 == 0 else 1)
        x1 = _out_proj(o_fox, o_m, opool, xp, mod_p, l, w_out_b, tm, False)
        if l % 2 == 0:
            j = l // 2
            xp = _ffn(x1, mod_p, l, nw2, wfg[j], wfu[j], wfd[j], fw, tm, tf, False, last)
        else:
            j = l // 2
            xp = _moe(x1, mod_p, l, nw2, w_router_p[j], weg[j], weu[j], wed[j], fw, tm, 256, False, last)
        c_fin = jnp.stack([c_pair[:, p, e * HEAD_DIM:(e + 1) * HEAD_DIM, e * HEAD_DIM:(e + 1) * HEAD_DIM]
                           for p in range(HEADS // 2) for e in range(2)], axis=1)
        st_p.append((kf.reshape(b, s, HEADS, HEAD_DIM), vf.reshape(b, s, HEADS, HEAD_DIM), lff[:, :, :HEADS],
                     c_fin, n_pair.reshape(b, HEADS, HEAD_DIM), m_fin[:, 0, :HEADS],
                     ulast[:, POOL_HALO - POOL_BUF:]))

        z, lff_s, li_s, lfm_s = _in_proj_sample(xs[0], mod_s, l, nw1, w_in_p, gate_b)
        seg = lambda i: z[:, i * MIX_W:(i + 1) * MIX_W]
        qf_s, kf_s, vf_s, qm_s, km_s, vm_s, om_s = (seg(i) for i in range(7))
        u_s = z[:, 7 * MIX_W:]
        col = lambda a: a.reshape(nb, MIX_W, 1)
        o_fox_s = _fox_decode(page_table, col(qf_s), col(kf_s), col(vf_s),
                              lff_s[:, :HEADS].reshape(nb, HEADS, 1, 1), k_t, v_t, floc, l,
                              min(32, page_table.shape[1]))
        hcol = lambda a: a.reshape(nb, HEADS, HEAD_DIM, 1)
        hrow = lambda a: a.reshape(nb, HEADS, 1, HEAD_DIM)
        hscl = lambda a: a[:, :HEADS].reshape(nb, HEADS, 1, 1)
        o_m_s, c_new, n_new, m_new = _mlstm_decode(
            hcol(qm_s), hcol(km_s), hrow(vm_s), hrow(om_s), hscl(li_s), hscl(lfm_s),
            state_mlstm_C[l], hcol(state_mlstm_n[l]), state_mlstm_m[l].reshape(nb, HEADS, 1, 1), mnw_row[l])
        o_pool_s, buf_new = _pool_decode(pool_t, u_s, wbd, pscale, l, page_table.shape[1] * page)
        mix = lambda a: a.astype(BF16).reshape(1, nb, -1)
        x1s = _out_proj(mix(o_fox_s), mix(o_m_s), mix(o_pool_s), xs, mod_s, l, w_out_b, nb, True)
        if l % 2 == 0:
            xs = _ffn(x1s, mod_s, l, nw2, wfg[j], wfu[j], wfd[j], fw, nb, tf, True, last)
        else:
            xs = _moe(x1s, mod_s, l, nw2, w_router_p[j], weg[j], weu[j], wed[j], fw, nb, 8, True, last)
        st_s.append((kf_s.reshape(nb, 1, HEADS, HEAD_DIM), vf_s.reshape(nb, 1, HEADS, HEAD_DIM),
                     lff_s[:, :HEADS].reshape(nb, 1, HEADS), c_new, n_new.reshape(nb, HEADS, HEAD_DIM),
                     m_new.reshape(nb, HEADS), jnp.transpose(buf_new, (1, 0, 2))))

    outs_p = [jnp.stack(a) for a in zip(*st_p)]
    outs_s = [jnp.stack(a) for a in zip(*st_s)]
    return (xp, xs.reshape(nb, 1, d), *outs_p, *outs_s)
```

```python
import functools

import jax
import jax.numpy as jnp
from jax import lax
from jax.experimental import pallas as pl
from jax.experimental.pallas import tpu as pltpu

F32 = jnp.float32
BF16 = jnp.bfloat16

HEAD_DIM = 64
LANES = 128
HEADS = 6
MIX_W = HEADS * HEAD_DIM
POOL_WINDOWS = (2, 4, 8, 16)
POOL_W = 256
POOL_HALO = 16
POOL_BUF = 15
MLSTM_CHUNK = 128
N_EXPERTS = 8
EPS = 1e-6
NEG_INF = float("-inf")

_MAIN_W = 7 * MIX_W + POOL_W
_GATE_OFF = _MAIN_W
IN_PAD_W = _MAIN_W + 3 * LANES

VMEM_LIMIT = 56 * 1024 * 1024


def _cparams(sem):
    return pltpu.CompilerParams(dimension_semantics=sem, vmem_limit_bytes=VMEM_LIMIT)


def _sigmoid(x):
    return 1.0 / (1.0 + jnp.exp(-x))


def _log_sigmoid(x):
    return jnp.minimum(x, 0.0) - jnp.log(1.0 + jnp.exp(-jnp.abs(x)))


def _split3(x):
    hi = x.astype(BF16).astype(F32)
    r = x - hi
    mid = r.astype(BF16).astype(F32)
    lo = (r - mid).astype(BF16).astype(F32)
    return hi, mid, lo


def _dot(a, b):
    return jnp.dot(a, b, preferred_element_type=F32)


def _dot_nt(a, b):
    return lax.dot_general(a, b, (((1,), (1,)), ((), ())), preferred_element_type=F32)


def _dot_tn(a, b):
    return lax.dot_general(a, b, (((0,), (0,)), ((), ())), preferred_element_type=F32)


def _tri_dot(tri_bf16, x):
    hi, mid, lo = _split3(x)
    return (_dot(tri_bf16, hi.astype(BF16)) + _dot(tri_bf16, mid.astype(BF16))
            + _dot(tri_bf16, lo.astype(BF16)))


def _dot_tri(x, tri_bf16):
    hi, mid, lo = _split3(x)
    return (_dot(hi.astype(BF16), tri_bf16) + _dot(mid.astype(BF16), tri_bf16)
            + _dot(lo.astype(BF16), tri_bf16))


def _dot_f32(a, b):
    a0, a1, a2 = (p.astype(BF16) for p in _split3(a))
    b0, b1, b2 = (p.astype(BF16) for p in _split3(b))
    small = _dot(a0, b2) + _dot(a2, b0) + _dot(a1, b1)
    return (small + (_dot(a0, b1) + _dot(a1, b0))) + _dot(a0, b0)


def _mm(a, w):
    if w.dtype == F32:
        return _dot_f32(a.astype(F32), w)
    return _dot(a.astype(BF16), w)


def _lower_tri(n):
    r = lax.broadcasted_iota(jnp.int32, (n, n), 0)
    c = lax.broadcasted_iota(jnp.int32, (n, n), 1)
    return r >= c


def _modulated_rmsnorm(x, nw, sc, sh):
    y = x * lax.rsqrt(jnp.mean(x * x, axis=-1, keepdims=True) + EPS)
    return (y * nw) * (1.0 + sc) + sh


def _ada_kernel(c_ref, w_ref, b_ref, o_ref):
    c = c_ref[...]
    s = c * _sigmoid(c)
    o_ref[...] = _mm(s, w_ref[...]) + b_ref[...]


def _ada_modulation(c_all, w_ada, b_ada):
    depth, d, _ = w_ada.shape
    r = c_all.shape[0]
    return pl.pallas_call(
        _ada_kernel,
        out_shape=jax.ShapeDtypeStruct((depth, 6, r, d), F32),
        grid=(depth, 6),
        in_specs=[
            pl.BlockSpec((r, d), lambda l, j: (0, 0)),
            pl.BlockSpec((None, d, d), lambda l, j: (l, 0, j)),
            pl.BlockSpec((None, 1, d), lambda l, j: (l, 0, j)),
        ],
        out_specs=pl.BlockSpec((None, None, r, d), lambda l, j: (l, j, 0, 0)),
        compiler_params=_cparams(("arbitrary", "arbitrary")),
        name="ada_modulation",
    )(c_all, w_ada, b_ada.reshape(depth, 1, 6 * d))


def _pool_mix(u, halo, pos0, wbd, scale):
    t = u.shape[0]
    ext = jnp.concatenate([halo, u], axis=0)
    p2 = ext + pltpu.roll(ext, 1, 0)
    p4 = p2 + pltpu.roll(p2, 2, 0)
    p8 = p4 + pltpu.roll(p4, 4, 0)
    p16 = p8 + pltpu.roll(p8, 8, 0)
    lane = lax.broadcasted_iota(jnp.int32, (t, POOL_W), 1)
    sums = jnp.where(lane < 64, p2[POOL_HALO:], jnp.where(lane < 128, p4[POOL_HALO:],
                     jnp.where(lane < 192, p8[POOL_HALO:], p16[POOL_HALO:])))
    win = jnp.where(lane < 64, 2.0, jnp.where(lane < 128, 4.0, jnp.where(lane < 192, 8.0, 16.0)))
    pos = (pos0 + lax.broadcasted_iota(jnp.int32, (t, POOL_W), 0)).astype(F32)
    cnt = jnp.minimum(pos + 1.0, win)
    d = sums / cnt - u
    return _dot(d.astype(BF16), wbd) * scale


def _in_proj_prompt_kernel(x_ref, nw_ref, sc_ref, sh_ref, w_ref, gb_ref, wbd_ref, ps_ref,
                           qa_ref, ka_ref, va_ref, kf_ref, vf_ref, lff_ref,
                           qm_ref, km_ref, vm_ref, om_ref, li_ref, lfm_ref, op_ref, ul_ref,
                           fcarry, uhalo):
    t = pl.program_id(1)
    tm = x_ref.shape[0]

    @pl.when(t == 0)
    def _():
        fcarry[...] = jnp.zeros_like(fcarry)
        uhalo[...] = jnp.zeros_like(uhalo)

    hb = _modulated_rmsnorm(x_ref[...], nw_ref[...], sc_ref[...], sh_ref[...]).astype(BF16)

    lane = lax.broadcasted_iota(jnp.int32, (tm, LANES), 1)
    zg = _dot(hb, w_ref[:, _GATE_OFF:IN_PAD_W])
    head_lane = lane < HEADS
    lf_fox = jnp.where(head_lane, _log_sigmoid(zg[:, 0:LANES] + gb_ref[0:1, :]), 0.0)
    li = jnp.where(head_lane, zg[:, LANES:2 * LANES] + gb_ref[1:2, :], 0.0)
    lf_m = jnp.where(head_lane, _log_sigmoid(zg[:, 2 * LANES:3 * LANES] + gb_ref[2:3, :]), 0.0)
    lff_ref[...] = lf_fox
    li_ref[...] = li
    lfm_ref[...] = lf_m

    cum = _tri_dot(_lower_tri(tm).astype(BF16), lf_fox) + fcarry[...]
    fcarry[...] = cum[tm - 1:tm, :]

    zq = _dot(hb, w_ref[:, 0:MIX_W]) * (HEAD_DIM ** -0.5)
    zk = _dot(hb, w_ref[:, MIX_W:2 * MIX_W])
    zv = _dot(hb, w_ref[:, 2 * MIX_W:3 * MIX_W])
    kf_ref[...] = zk
    vf_ref[...] = zv
    low = lane < HEAD_DIM
    for h in range(HEADS):
        p, e = divmod(h, 2)
        sl = slice(p * LANES, (p + 1) * LANES)
        bq, bk, bv = zq[:, sl], zk[:, sl], zv[:, sl]
        if e == 1:
            bq, bk, bv = (pltpu.roll(a, HEAD_DIM, 1) for a in (bq, bk, bv))
        hi, mid, lo = _split3(cum[:, h:h + 1])
        augq = jnp.where(lane == 64, hi, jnp.where(lane == 65, mid, jnp.where(lane == 66, lo,
                         jnp.where(lane < 70, 1.0, 0.0))))
        augk = jnp.where(lane < 67, 1.0, jnp.where(lane == 67, -hi, jnp.where(lane == 68, -mid,
                         jnp.where(lane == 69, -lo, 0.0))))
        augv = jnp.where(lane == 64, 1.0, 0.0)
        qa_ref[h] = jnp.where(low, bq, augq).astype(BF16)
        ka_ref[h] = jnp.where(low, bk, augk).astype(BF16)
        va_ref[h] = jnp.where(low, bv, augv).astype(BF16)

    o = 3 * MIX_W
    qm_ref[...] = _dot(hb, w_ref[:, o:o + MIX_W]).astype(BF16)
    km_ref[...] = (_dot(hb, w_ref[:, o + MIX_W:o + 2 * MIX_W]) * (HEAD_DIM ** -0.5)).astype(BF16)
    vm_ref[...] = _dot(hb, w_ref[:, o + 2 * MIX_W:o + 3 * MIX_W]).astype(BF16)
    om_ref[...] = _dot(hb, w_ref[:, o + 3 * MIX_W:o + 4 * MIX_W])

    u = _dot(hb, w_ref[:, 7 * MIX_W:_MAIN_W])
    op_ref[...] = _pool_mix(u, uhalo[...], t * tm, wbd_ref[...], ps_ref[...]).astype(BF16)
    uhalo[...] = u[tm - POOL_HALO:, :]
    ul_ref[...] = u[tm - POOL_HALO:, :]


def _in_proj_prompt(x, mod, l, nw, w_in_p, gate_b, wbd, pscale, tm):
    b, s, d = x.shape
    nt = s // tm
    row = lambda width, dt: jax.ShapeDtypeStruct((b, s, width), dt)
    head = jax.ShapeDtypeStruct((b, HEADS, s, LANES), BF16)
    tok = lambda width: pl.BlockSpec((None, tm, width), lambda i, t: (i, t, 0))
    hspec = pl.BlockSpec((None, HEADS, tm, LANES), lambda i, t: (i, 0, t, 0))
    modspec = lambda j: pl.BlockSpec((None, None, None, 1, d), lambda i, t: (l, j, i, 0, 0))
    return pl.pallas_call(
        _in_proj_prompt_kernel,
        out_shape=(head, head, head, row(MIX_W, F32), row(MIX_W, F32), row(LANES, F32),
                   row(MIX_W, BF16), row(MIX_W, BF16), row(MIX_W, BF16), row(MIX_W, F32),
                   row(LANES, F32), row(LANES, F32), row(POOL_W, BF16),
                   jax.ShapeDtypeStruct((b, POOL_HALO, POOL_W), F32)),
        grid=(b, nt),
        in_specs=[
            tok(d),
            pl.BlockSpec((None, 1, d), lambda i, t: (l, 0, 0)),
            modspec(1), modspec(0),
            pl.BlockSpec((None, d, IN_PAD_W), lambda i, t: (l, 0, 0)),
            pl.BlockSpec((None, 3, LANES), lambda i, t: (l, 0, 0)),
            pl.BlockSpec((None, POOL_W, POOL_W), lambda i, t: (l, 0, 0)),
            pl.BlockSpec((None, 1, POOL_W), lambda i, t: (l, 0, 0)),
        ],
        out_specs=(hspec, hspec, hspec, tok(MIX_W), tok(MIX_W), tok(LANES),
                   tok(MIX_W), tok(MIX_W), tok(MIX_W), tok(MIX_W), tok(LANES), tok(LANES),
                   tok(POOL_W), pl.BlockSpec((None, POOL_HALO, POOL_W), lambda i, t: (i, 0, 0))),
        scratch_shapes=[pltpu.VMEM((1, LANES), F32), pltpu.VMEM((POOL_HALO, POOL_W), F32)],
        compiler_params=_cparams(("arbitrary", "arbitrary")),
        name="in_proj_prompt",
    )(x, nw, mod, mod, w_in_p, gate_b, wbd, pscale)


def _in_proj_sample_kernel(x_ref, nw_ref, sc_ref, sh_ref, w_ref, gb_ref, z_ref, lff_ref, li_ref, lfm_ref):
    h = _modulated_rmsnorm(x_ref[...], nw_ref[...], sc_ref[...], sh_ref[...])
    chunk = 4 * LANES
    for c0 in range(0, _MAIN_W, chunk):
        c1 = min(c0 + chunk, _MAIN_W)
        z_ref[:, c0:c1] = _mm(h, w_ref[:, c0:c1])
    zg = _mm(h, w_ref[:, _GATE_OFF:IN_PAD_W])
    lff_ref[...] = _log_sigmoid(zg[:, 0:LANES] + gb_ref[0:1, :])
    li_ref[...] = zg[:, LANES:2 * LANES] + gb_ref[1:2, :]
    lfm_ref[...] = _log_sigmoid(zg[:, 2 * LANES:3 * LANES] + gb_ref[2:3, :])


def _in_proj_sample(x, mod, l, nw, w_in_p, gate_b):
    n, d = x.shape
    gate = jax.ShapeDtypeStruct((n, LANES), F32)
    full = lambda *shape: pl.BlockSpec(shape, lambda i: (0,) * len(shape))
    modspec = lambda j: pl.BlockSpec((None, None, n, d), lambda i: (l, j, 0, 0))
    return pl.pallas_call(
        _in_proj_sample_kernel,
        out_shape=(jax.ShapeDtypeStruct((n, _MAIN_W), F32), gate, gate, gate),
        grid=(1,),
        in_specs=[
            full(n, d),
            pl.BlockSpec((None, 1, d), lambda i: (l, 0, 0)),
            modspec(1), modspec(0),
            pl.BlockSpec((None, d, IN_PAD_W), lambda i: (l, 0, 0)),
            pl.BlockSpec((None, 3, LANES), lambda i: (l, 0, 0)),
        ],
        out_specs=(full(n, _MAIN_W), full(n, LANES), full(n, LANES), full(n, LANES)),
        compiler_params=_cparams(("arbitrary",)),
        name="in_proj_sample",
    )(x, nw, mod, mod, w_in_p, gate_b)


def _fox_prompt_kernel(qa_ref, ka_ref, va_ref, o_ref, *, tq):
    qi = pl.program_id(2)
    q = (qa_ref[0], qa_ref[1])

    def block(e, ki, carry, diag):
        m, acc = carry
        k = ka_ref[e, pl.ds(pl.multiple_of(ki * tq, tq), tq), :]
        v = va_ref[e, pl.ds(pl.multiple_of(ki * tq, tq), tq), :]
        s = _dot_nt(q[e], k)
        if diag:
            s = jnp.where(_lower_tri(tq), s, NEG_INF)
        m_new = jnp.maximum(m, jnp.max(s, axis=-1, keepdims=True))
        p = jnp.exp(s - m_new)
        acc = acc * jnp.exp(m - m_new) + _dot(p.astype(BF16), v)
        return m_new, acc

    def body(ki, carry):
        return block(0, ki, carry[0], False), block(1, ki, carry[1], False)

    init = (jnp.full((tq, 1), NEG_INF, F32), jnp.zeros((tq, LANES), F32))
    carry = lax.fori_loop(0, qi, body, (init, init))
    outs = []
    for e in range(2):
        _, acc = block(e, qi, carry[e], True)
        outs.append(acc / acc[:, HEAD_DIM:HEAD_DIM + 1])
    lane = lax.broadcasted_iota(jnp.int32, (tq, LANES), 1)
    o_ref[...] = jnp.where(lane < HEAD_DIM, outs[0], pltpu.roll(outs[1], HEAD_DIM, 1)).astype(o_ref.dtype)


def _fox_prompt(qa, ka, va, tq):
    b, _, s, _ = qa.shape
    return pl.pallas_call(
        functools.partial(_fox_prompt_kernel, tq=tq),
        out_shape=jax.ShapeDtypeStruct((b, s, MIX_W), BF16),
        grid=(b, HEADS // 2, s // tq),
        in_specs=[
            pl.BlockSpec((None, 2, tq, LANES), lambda i, p, t: (i, p, t, 0)),
            pl.BlockSpec((None, 2, s, LANES), lambda i, p, t: (i, p, 0, 0)),
            pl.BlockSpec((None, 2, s, LANES), lambda i, p, t: (i, p, 0, 0)),
        ],
        out_specs=pl.BlockSpec((None, tq, LANES), lambda i, p, t: (i, t, p)),
        compiler_params=_cparams(("arbitrary", "arbitrary", "arbitrary")),
        name="fox_prompt",
    )(qa, ka, va)


def _mlstm_prompt_kernel(q_ref, k_ref, v_ref, og_ref, li_ref, lf_ref, nw_ref,
                         o_ref, c_out, n_out, m_out, c_st, n_st, m_st):
    c = pl.program_id(1)
    nb, L, _ = q_ref.shape

    @pl.when(c == 0)
    def _():
        c_st[...] = jnp.zeros_like(c_st)
        n_st[...] = jnp.zeros_like(n_st)
        m_st[...] = jnp.zeros_like(m_st)

    tril = _lower_tri(L)
    tri = tril.astype(BF16)
    lane = lax.broadcasted_iota(jnp.int32, (L, LANES), 1)
    lane1 = lax.broadcasted_iota(jnp.int32, (1, LANES), 1)
    rowi = lax.broadcasted_iota(jnp.int32, (LANES, LANES), 0)
    coli = lax.broadcasted_iota(jnp.int32, (LANES, LANES), 1)
    same_head = (rowi < HEAD_DIM) == (coli < HEAD_DIM)

    for bi in range(nb):
        lf = lf_ref[bi]
        li = li_ref[bi]
        bcum = _tri_dot(tri, lf)
        m_prev = m_st[bi]
        b_end = bcum[L - 1:L, :]
        g = b_end - bcum + li
        m_new = jnp.maximum(b_end + m_prev, jnp.max(g, axis=0, keepdims=True))
        a_state = jnp.exp(b_end + m_prev - m_new)
        wg = jnp.exp(g - m_new)
        inter = bcum + m_prev
        bcum_t = bcum.T
        li_t = li.T
        heads = range(HEADS)
        pairs = [slice(p * LANES, (p + 1) * LANES) for p in range(HEADS // 2)]
        qs = [q_ref[bi, :, sl] for sl in pairs]
        ks = [k_ref[bi, :, sl] for sl in pairs]
        vs = [v_ref[bi, :, sl] for sl in pairs]
        mine = [(lane < HEAD_DIM) if h % 2 == 0 else (lane >= HEAD_DIM) for h in heads]
        own = jnp.stack(mine)
        q6 = jnp.stack([jnp.where(mine[h], qs[h // 2], jnp.zeros_like(qs[0])) for h in heads])
        k6 = jnp.stack([ks[h // 2] for h in heads])
        v6 = jnp.stack([vs[h // 2] for h in heads])
        n6 = jnp.stack([n_st[bi, h // 2] for h in heads])
        qc3 = [_dot(qs[p], c_st[bi, p].astype(BF16)) for p in range(HEADS // 2)]
        qc6 = jnp.stack([qc3[h // 2] for h in heads])
        bcol = jnp.stack([bcum[:, h:h + 1] for h in heads])
        icol = jnp.stack([inter[:, h:h + 1] for h in heads])
        brow = jnp.stack([bcum_t[h:h + 1, :] for h in heads])
        lrow = jnp.stack([li_t[h:h + 1, :] for h in heads])
        dmat = jnp.where(tril[None], bcol - brow + lrow, NEG_INF)
        m_t = jnp.maximum(icol, jnp.max(dmat, axis=-1, keepdims=True))
        a_int = jnp.exp(icol - m_t)
        s = lax.dot_general(q6, k6, (((2,), (2,)), ((0,), (0,))), preferred_element_type=F32) \
            * jnp.exp(dmat - m_t)
        sv = lax.dot_general(s.astype(BF16), v6, (((2,), (1,)), ((0,), (0,))), preferred_element_type=F32)
        den = a_int * jnp.sum(q6.astype(F32) * n6, axis=-1, keepdims=True) + jnp.sum(s, axis=-1, keepdims=True)
        hh = (a_int * qc6 + sv) / jnp.maximum(jnp.abs(den), jnp.exp(-m_t))
        ms = jnp.sum(jnp.where(own, hh * hh, 0.0), axis=-1, keepdims=True) * (1.0 / HEAD_DIM)
        hn = hh * lax.rsqrt(ms + EPS)
        for p in range(HEADS // 2):
            sl = pairs[p]
            k, v = ks[p], vs[p]
            cp = c_st[bi, p]
            npair = n_st[bi, p]
            hpair = jnp.where(lane < HEAD_DIM, hn[2 * p], hn[2 * p + 1])
            o_ref[bi, :, sl] = (_sigmoid(og_ref[bi, :, sl]) * (hpair * nw_ref[:, sl])).astype(o_ref.dtype)
            wgp = jnp.where(lane < HEAD_DIM, wg[:, 2 * p:2 * p + 1], wg[:, 2 * p + 1:2 * p + 2])
            ap = jnp.where(lane1 < HEAD_DIM, a_state[:, 2 * p:2 * p + 1], a_state[:, 2 * p + 1:2 * p + 2])
            kv = _dot_tn(k, (v.astype(F32) * wgp).astype(BF16))
            c_st[bi, p] = cp * ap + jnp.where(same_head, kv, 0.0)
            n_st[bi, p] = npair * ap + jnp.sum(k.astype(F32) * wgp, axis=0, keepdims=True)
        m_st[bi] = m_new

    @pl.when(c == pl.num_programs(1) - 1)
    def _():
        c_out[...] = c_st[...]
        n_out[...] = n_st[...]
        m_out[...] = m_st[...]


def _mlstm_prompt(qm, km, vm, om, li, lfm, nw, l, nb):
    b, s, _ = qm.shape
    L = MLSTM_CHUNK
    np_ = HEADS // 2
    tok = lambda w: pl.BlockSpec((nb, L, w), lambda i, c: (i, c, 0))
    return pl.pallas_call(
        _mlstm_prompt_kernel,
        out_shape=(jax.ShapeDtypeStruct((b, s, MIX_W), BF16),
                   jax.ShapeDtypeStruct((b, np_, LANES, LANES), F32),
                   jax.ShapeDtypeStruct((b, np_, 1, LANES), F32),
                   jax.ShapeDtypeStruct((b, 1, LANES), F32)),
        grid=(b // nb, s // L),
        in_specs=[tok(MIX_W), tok(MIX_W), tok(MIX_W), tok(MIX_W), tok(LANES), tok(LANES),
                  pl.BlockSpec((None, 1, MIX_W), lambda i, c: (l, 0, 0))],
        out_specs=(tok(MIX_W),
                   pl.BlockSpec((nb, np_, LANES, LANES), lambda i, c: (i, 0, 0, 0)),
                   pl.BlockSpec((nb, np_, 1, LANES), lambda i, c: (i, 0, 0, 0)),
                   pl.BlockSpec((nb, 1, LANES), lambda i, c: (i, 0, 0))),
        scratch_shapes=[pltpu.VMEM((nb, np_, LANES, LANES), F32),
                        pltpu.VMEM((nb, np_, 1, LANES), F32),
                        pltpu.VMEM((nb, 1, LANES), F32)],
        compiler_params=_cparams(("arbitrary", "arbitrary")),
        name="mlstm_prompt",
    )(qm, km, vm, om, li, lfm, nw)


def _out_proj_kernel(of_ref, om_ref, op_ref, x_ref, g_ref, w_ref, o_ref):
    y = (_mm(of_ref[...], w_ref[0:MIX_W, :]) + _mm(om_ref[...], w_ref[MIX_W:2 * MIX_W, :])
         + _mm(op_ref[...], w_ref[2 * MIX_W:, :]))
    o_ref[...] = x_ref[...] + g_ref[...] * y


def _out_proj(of, om, op, x, mod, l, w_out_b, tm, per_row_mod):
    b, s, d = x.shape
    tok = lambda w: pl.BlockSpec((None, tm, w), lambda i, t: (i, t, 0))
    if per_row_mod:
        gspec = pl.BlockSpec((None, None, tm, d), lambda i, t: (l, 2, 0, 0))
    else:
        gspec = pl.BlockSpec((None, None, None, 1, d), lambda i, t: (l, 2, i, 0, 0))
    return pl.pallas_call(
        _out_proj_kernel,
        out_shape=jax.ShapeDtypeStruct((b, s, d), F32),
        grid=(b, s // tm),
        in_specs=[tok(MIX_W), tok(MIX_W), tok(POOL_W), tok(d), gspec,
                  pl.BlockSpec((None, d, d), lambda i, t: (l, 0, 0))],
        out_specs=tok(d),
        compiler_params=_cparams(("arbitrary", "arbitrary")),
        name="out_proj",
    )(of, om, op, x, mod, w_out_b)


def _final_norm(x, fw):
    return (x * lax.rsqrt(jnp.mean(x * x, axis=-1, keepdims=True) + EPS)) * fw


def _ffn_kernel(x_ref, nw_ref, sc_ref, sh_ref, g_ref, wg_ref, wu_ref, wd_ref, fw_ref, o_ref,
                h_sc, acc, *, final):
    j = pl.program_id(2)

    @pl.when(j == 0)
    def _():
        h_sc[...] = _modulated_rmsnorm(x_ref[...], nw_ref[...], sc_ref[...], sh_ref[...]).astype(h_sc.dtype)
        acc[...] = jnp.zeros_like(acc)

    h = h_sc[...]
    a = _mm(h, wg_ref[...])
    u = _mm(h, wu_ref[...])
    acc[...] += _mm((a * _sigmoid(a)) * u, wd_ref[...])

    @pl.when(j == pl.num_programs(2) - 1)
    def _():
        y = x_ref[...] + g_ref[...] * acc[...]
        o_ref[...] = _final_norm(y, fw_ref[...]) if final else y


def _mod_specs(l, d, tm, per_row_mod, nidx):
    def spec(j):
        if per_row_mod:
            return pl.BlockSpec((None, None, tm, d), lambda *a: (l, j, 0, 0))
        return pl.BlockSpec((None, None, None, 1, d), lambda *a: (l, j, a[0], 0, 0))
    return spec(4), spec(3), spec(5)


def _ffn(x, mod, l, nw, wg, wu, wd, fw, tm, tf, per_row_mod, final):
    b, s, d = x.shape
    ff = wg.shape[-1]
    sc, sh, g = _mod_specs(l, d, tm, per_row_mod, 3)
    return pl.pallas_call(
        functools.partial(_ffn_kernel, final=final),
        out_shape=jax.ShapeDtypeStruct((b, s, d), F32),
        grid=(b, s // tm, ff // tf),
        in_specs=[
            pl.BlockSpec((None, tm, d), lambda i, t, j: (i, t, 0)),
            pl.BlockSpec((None, 1, d), lambda i, t, j: (l, 0, 0)),
            sc, sh, g,
            pl.BlockSpec((d, tf), lambda i, t, j: (0, j)),
            pl.BlockSpec((d, tf), lambda i, t, j: (0, j)),
            pl.BlockSpec((tf, d), lambda i, t, j: (j, 0)),
            pl.BlockSpec((1, d), lambda i, t, j: (0, 0)),
        ],
        out_specs=pl.BlockSpec((None, tm, d), lambda i, t, j: (i, t, 0)),
        scratch_shapes=[pltpu.VMEM((tm, d), wg.dtype), pltpu.VMEM((tm, d), F32)],
        compiler_params=_cparams(("arbitrary", "arbitrary", "arbitrary")),
        name="ffn_swiglu",
    )(x, nw, mod, mod, mod, wg, wu, wd, fw)


def _top2(logits):
    lane = lax.broadcasted_iota(jnp.int32, logits.shape, 1)
    valid = lane < N_EXPERTS
    z = jnp.where(valid, logits, NEG_INF)
    pz = jnp.exp(z - jnp.max(z, axis=-1, keepdims=True))
    probs = pz / jnp.sum(pz, axis=-1, keepdims=True)
    m1 = jnp.max(probs, axis=-1, keepdims=True)
    i1 = jnp.min(jnp.where(probs == m1, lane, LANES), axis=-1, keepdims=True)
    rest = jnp.where((lane == i1) | (~valid), -1.0, probs)
    m2 = jnp.max(rest, axis=-1, keepdims=True)
    i2 = jnp.min(jnp.where(rest == m2, lane, LANES), axis=-1, keepdims=True)
    tot = m1 + m2
    return i1, i2, m1 / tot, m2 / tot


_R_E1, _R_E2, _R_R1, _R_R2, _R_G1, _R_G2 = range(6)


def _moe_route_kernel(x_ref, nw_ref, sc_ref, sh_ref, wr_ref, h_ref, route_ref, cnt_ref, carry):
    first = (pl.program_id(0) == 0) & (pl.program_id(1) == 0)

    @pl.when(first)
    def _():
        carry[...] = jnp.zeros_like(carry)

    h = _modulated_rmsnorm(x_ref[...], nw_ref[...], sc_ref[...], sh_ref[...])
    h_ref[...] = h
    tm = h.shape[0]
    i1, i2, g1, g2 = _top2(_dot_f32(h, wr_ref[...]))
    lane = lax.broadcasted_iota(jnp.int32, (tm, LANES), 1)
    oh1 = lane == i1
    oh2 = lane == i2
    both = jnp.where(oh1 | oh2, 1.0, 0.0)
    r = lax.broadcasted_iota(jnp.int32, (tm, tm), 0)
    c = lax.broadcasted_iota(jnp.int32, (tm, tm), 1)
    before = carry[...] + _dot((r > c).astype(BF16), both.astype(BF16))
    r1 = jnp.sum(jnp.where(oh1, before, 0.0), axis=-1, keepdims=True)
    r2 = jnp.sum(jnp.where(oh2, before, 0.0), axis=-1, keepdims=True)
    carry[...] += jnp.sum(both, axis=0, keepdims=True)
    cnt_ref[...] = carry[...]
    vals = (i1.astype(F32), i2.astype(F32), r1, r2, g1, g2)
    route = jnp.zeros((tm, LANES), F32)
    for k, v in enumerate(vals):
        route = jnp.where(lane == k, v, route)
    route_ref[...] = route


def _moe_route(x, mod, l, nw, wr, tm, per_row_mod):
    b, s, d = x.shape
    sc, sh, _ = _mod_specs(l, d, tm, per_row_mod, 2)
    tok = lambda w: pl.BlockSpec((None, tm, w), lambda i, t: (i, t, 0))
    return pl.pallas_call(
        _moe_route_kernel,
        out_shape=(jax.ShapeDtypeStruct((b, s, d), F32), jax.ShapeDtypeStruct((b, s, LANES), F32),
                   jax.ShapeDtypeStruct((1, LANES), F32)),
        grid=(b, s // tm),
        in_specs=[tok(d), pl.BlockSpec((None, 1, d), lambda i, t: (l, 0, 0)), sc, sh,
                  pl.BlockSpec((d, LANES), lambda i, t: (0, 0))],
        out_specs=(tok(d), tok(LANES), pl.BlockSpec((1, LANES), lambda i, t: (0, 0))),
        scratch_shapes=[pltpu.VMEM((1, LANES), F32)],
        compiler_params=_cparams(("arbitrary", "arbitrary")),
        name="moe_route",
    )(x, nw, mod, mod, wr)


def _row_copy(src, src_row, dst, dst_row, sem):
    return pltpu.make_async_copy(src.at[pl.ds(src_row, 1)], dst.at[pl.ds(dst_row, 1)], sem)


def _moe_dispatch_kernel(zrow_ref, d1_ref, d2_ref, h_ref, hs_ref, zbuf, sem, zsem):
    tm = h_ref.shape[0]
    tg = zbuf.shape[0]

    @pl.when(pl.program_id(0) == 0)
    def _():
        zbuf[...] = jnp.zeros_like(zbuf)
        fills = [pltpu.make_async_copy(
            zbuf, hs_ref.at[pl.ds(pl.multiple_of(jnp.maximum(zrow_ref[e], 0), tg), tg)], zsem)
            for e in range(N_EXPERTS)]
        for e in range(N_EXPERTS):
            @pl.when(zrow_ref[e] >= 0)
            def _():
                fills[e].start()
        for e in range(N_EXPERTS):
            @pl.when(zrow_ref[e] >= 0)
            def _():
                fills[e].wait()

        def tail(t):
            return pltpu.make_async_copy(zbuf, hs_ref.at[pl.ds(pl.multiple_of(t * tg, tg), tg)], zsem)

        n_tiles = hs_ref.shape[0] // tg
        lax.fori_loop(zrow_ref[N_EXPERTS], n_tiles, lambda t, c: (tail(t).start(), c)[1], 0)
        lax.fori_loop(zrow_ref[N_EXPERTS], n_tiles, lambda t, c: (tail(t).wait(), c)[1], 0)

    def start(r, _):
        _row_copy(h_ref, r, hs_ref, d1_ref[0, r], sem).start()
        _row_copy(h_ref, r, hs_ref, d2_ref[0, r], sem).start()
        return 0

    def wait(r, _):
        _row_copy(h_ref, r, hs_ref, d1_ref[0, r], sem).wait()
        _row_copy(h_ref, r, hs_ref, d2_ref[0, r], sem).wait()
        return 0

    lax.fori_loop(0, tm, start, 0, unroll=8)
    lax.fori_loop(0, tm, wait, 0, unroll=8)


def _moe_dispatch(h, dest1, dest2, zrow, rows, tm, tg):
    n, d = h.shape
    idx = pl.BlockSpec((None, 1, tm), lambda i, z: (i, 0, 0), memory_space=pltpu.SMEM)
    grid_spec = pltpu.PrefetchScalarGridSpec(
        num_scalar_prefetch=1,
        grid=(n // tm,),
        in_specs=[idx, idx, pl.BlockSpec((tm, d), lambda i, z: (i, 0))],
        out_specs=pl.BlockSpec(memory_space=pl.ANY),
        scratch_shapes=[pltpu.VMEM((tg, d), F32), pltpu.SemaphoreType.DMA, pltpu.SemaphoreType.DMA],
    )
    return pl.pallas_call(
        _moe_dispatch_kernel,
        out_shape=jax.ShapeDtypeStruct((rows, d), F32),
        grid_spec=grid_spec,
        compiler_params=_cparams(("arbitrary",)),
        name="moe_dispatch",
    )(zrow, dest1, dest2, h)


def _moe_experts_kernel(te_ref, nv_ref, x_ref, wg_ref, wu_ref, wd_ref, o_ref, *, halves):
    del te_ref
    i = pl.program_id(0)

    @pl.when(i < nv_ref[0])
    def _():
        x = x_ref[...].astype(BF16)
        ff = wg_ref.shape[-1]
        hw = ff // halves
        y = None
        for k in range(halves):
            sl = slice(k * hw, (k + 1) * hw)
            a = _dot(x, wg_ref[:, sl])
            u = _dot(x, wu_ref[:, sl])
            part = _dot(((a * _sigmoid(a)) * u).astype(BF16), wd_ref[sl, :])
            y = part if y is None else y + part
        o_ref[...] = y

    @pl.when(i >= nv_ref[0])
    def _():
        o_ref[...] = jnp.zeros_like(o_ref)


def _moe_experts(hs, tile_expert, n_valid, wg, wu, wd, tg):
    rows, d = hs.shape
    ff = wg.shape[-1]
    grid_spec = pltpu.PrefetchScalarGridSpec(
        num_scalar_prefetch=2,
        grid=(rows // tg,),
        in_specs=[pl.BlockSpec((tg, d), lambda i, te, nv: (i, 0)),
                  pl.BlockSpec((None, d, ff), lambda i, te, nv: (te[i], 0, 0)),
                  pl.BlockSpec((None, d, ff), lambda i, te, nv: (te[i], 0, 0)),
                  pl.BlockSpec((None, ff, d), lambda i, te, nv: (te[i], 0, 0))],
        out_specs=pl.BlockSpec((tg, d), lambda i, te, nv: (i, 0)),
    )
    return pl.pallas_call(
        functools.partial(_moe_experts_kernel, halves=2),
        out_shape=jax.ShapeDtypeStruct((rows, d), F32),
        grid_spec=grid_spec,
        compiler_params=_cparams(("arbitrary",)),
        name="moe_experts",
    )(tile_expert, n_valid, hs, wg, wu, wd)


def _moe_combine_kernel(d1_ref, d2_ref, x_ref, g_ref, route_ref, fw_ref, ys_ref, o_ref, buf1, buf2, sem,
                        *, final):
    tm = x_ref.shape[0]

    def start(r, _):
        _row_copy(ys_ref, d1_ref[0, r], buf1, r, sem).start()
        _row_copy(ys_ref, d2_ref[0, r], buf2, r, sem).start()
        return 0

    def wait(r, _):
        _row_copy(ys_ref, d1_ref[0, r], buf1, r, sem).wait()
        _row_copy(ys_ref, d2_ref[0, r], buf2, r, sem).wait()
        return 0

    lax.fori_loop(0, tm, start, 0, unroll=8)
    lax.fori_loop(0, tm, wait, 0, unroll=8)
    route = route_ref[...]
    f = route[:, _R_G1:_R_G1 + 1] * buf1[...] + route[:, _R_G2:_R_G2 + 1] * buf2[...]
    y = x_ref[...] + g_ref[...] * f
    o_ref[...] = _final_norm(y, fw_ref[...]) if final else y


def _moe_combine(x, mod, l, route, dest1, dest2, ys, fw, tm, per_row_mod, final):
    b, s, d = x.shape
    nt = s // tm
    _, _, g = _mod_specs(l, d, tm, per_row_mod, 2)
    idx = pl.BlockSpec((None, 1, tm), lambda i, t: (i * nt + t, 0, 0), memory_space=pltpu.SMEM)
    tok = lambda w: pl.BlockSpec((None, tm, w), lambda i, t: (i, t, 0))
    return pl.pallas_call(
        functools.partial(_moe_combine_kernel, final=final),
        out_shape=jax.ShapeDtypeStruct((b, s, d), F32),
        grid=(b, nt),
        in_specs=[idx, idx, tok(d), g, tok(LANES), pl.BlockSpec((1, d), lambda i, t: (0, 0)),
                  pl.BlockSpec(memory_space=pl.ANY)],
        out_specs=tok(d),
        scratch_shapes=[pltpu.VMEM((tm, d), F32), pltpu.VMEM((tm, d), F32), pltpu.SemaphoreType.DMA],
        compiler_params=_cparams(("arbitrary", "arbitrary")),
        name="moe_combine",
    )(dest1, dest2, x, mod, route, fw, ys)


def _moe(x, mod, l, nw, wr, wg, wu, wd, fw, tm, tg, per_row_mod, final):
    b, s, d = x.shape
    n = b * s
    h, route, counts = _moe_route(x, mod, l, nw, wr, tm, per_row_mod)
    cnt = counts[0, :N_EXPERTS].astype(jnp.int32)
    padded = (cnt + tg - 1) // tg * tg
    ends = jnp.cumsum(padded)
    offs = ends - padded
    rows = (2 * n + N_EXPERTS * (tg - 1) + tg - 1) // tg * tg
    n_valid = ends[-1] // tg
    tile = jnp.minimum(jnp.arange(rows // tg, dtype=jnp.int32), n_valid - 1)
    tile_expert = jnp.minimum(jnp.sum(tile[:, None] * tg >= ends[None, :], axis=1), N_EXPERTS - 1).astype(jnp.int32)
    rt = route.reshape(n, LANES)
    dest = lambda ke, kr: (offs[rt[:, ke].astype(jnp.int32)] + rt[:, kr].astype(jnp.int32)).reshape(n // tm, 1, tm)
    dest1, dest2 = dest(_R_E1, _R_R1), dest(_R_E2, _R_R2)
    zrow = jnp.concatenate([jnp.where(padded > 0, ends - tg, -1), n_valid[None]]).astype(jnp.int32)
    hs = _moe_dispatch(h.reshape(n, d), dest1, dest2, zrow, rows, tm, tg)
    ys = _moe_experts(hs, tile_expert, n_valid.reshape(1), wg, wu, wd, tg)
    return _moe_combine(x, mod, l, route, dest1, dest2, ys, fw, tm, per_row_mod, final)


def _page_cumsum_kernel(lf_ref, o_ref):
    n = lf_ref.shape[-1]
    r = lax.broadcasted_iota(jnp.int32, (n, n), 0)
    c = lax.broadcasted_iota(jnp.int32, (n, n), 1)
    o_ref[...] = _dot_tri(lf_ref[...], (r <= c).astype(BF16))


def _page_cumsum(lf_t):
    depth, h, n_pool, page = lf_t.shape
    return pl.pallas_call(
        _page_cumsum_kernel,
        out_shape=jax.ShapeDtypeStruct(lf_t.shape, F32),
        grid=(depth, h),
        in_specs=[pl.BlockSpec((None, None, n_pool, page), lambda l, i: (l, i, 0, 0))],
        out_specs=pl.BlockSpec((None, None, n_pool, page), lambda l, i: (l, i, 0, 0)),
        compiler_params=_cparams(("arbitrary", "arbitrary")),
        name="page_cumsum",
    )(lf_t)


def _fox_decode_kernel(pt_ref, q_ref, kn_ref, vn_ref, lfn_ref, k_hbm, v_hbm, f_hbm, o_ref,
                       pbuf, fbuf, psem, fsem, s_sc, acc, qb, *, l, nbuf):
    b = pl.program_id(0)
    n_pages = pt_ref.shape[1]
    page = pbuf.shape[-1]
    per_seq = 2 * n_pages
    total = pl.num_programs(0) * per_seq

    def page_copy(src, seq, j, slot):
        return pltpu.make_async_copy(src.at[l, pt_ref[seq, j]], pbuf.at[slot], psem.at[slot])

    def f_copy(seq, j, slot):
        pg = pt_ref[seq, j]
        src = f_hbm.at[l, :, lax.shift_right_logical(pg, 3), pl.ds(pg & 7, 1), :]
        return pltpu.make_async_copy(src, fbuf.at[slot], fsem.at[slot])

    def start_item(g, slot):
        seq = lax.div(g, per_seq)
        i = g - seq * per_seq

        @pl.when(i < n_pages)
        def _():
            page_copy(k_hbm, seq, i, slot).start()
            f_copy(seq, i, slot).start()

        @pl.when(i >= n_pages)
        def _():
            page_copy(v_hbm, seq, i - n_pages, slot).start()

    @pl.when(b == 0)
    def _():
        for s in range(nbuf):
            page_copy(k_hbm, 0, s, s).start()
            f_copy(0, s, s).start()

    q3 = q_ref[...].reshape(HEADS, HEAD_DIM, 1) * (HEAD_DIM ** -0.5)
    qb[...] = jnp.broadcast_to(q3, qb.shape)

    def k_body(i, run):
        slot = i & (nbuf - 1)
        page_copy(k_hbm, b, i, slot).wait()
        f_copy(b, i, slot).wait()
        floc = fbuf[slot]
        sc = jnp.sum(pbuf[slot] * qb[...], axis=1, keepdims=True)
        s_sc[:, :, pl.ds(pl.multiple_of(i * page, page), page)] = sc - (run + floc)
        nxt = b * per_seq + i + nbuf

        @pl.when(nxt < total)
        def _():
            start_item(nxt, slot)

        return run + floc[:, :, page - 1:page]

    run = lax.fori_loop(0, n_pages, k_body, jnp.zeros((HEADS, 1, 1), F32))

    s_all = s_sc[...]
    s_new = jnp.sum(q3 * kn_ref[...].reshape(HEADS, HEAD_DIM, 1), axis=1, keepdims=True) - (run + lfn_ref[...])
    m = jnp.maximum(jnp.max(s_all, axis=2, keepdims=True), s_new)
    p = jnp.exp(s_all - m)
    p_new = jnp.exp(s_new - m)
    s_sc[...] = p
    denom = jnp.sum(p, axis=2, keepdims=True) + p_new
    acc[...] = jnp.zeros_like(acc)

    def v_body(i, _):
        slot = i & (nbuf - 1)
        page_copy(v_hbm, b, i, slot).wait()
        acc[...] += pbuf[slot] * s_sc[:, :, pl.ds(pl.multiple_of(i * page, page), page)]
        nxt = b * per_seq + n_pages + i + nbuf

        @pl.when(nxt < total)
        def _():
            start_item(nxt, slot)

        return 0

    lax.fori_loop(0, n_pages, v_body, 0)
    o3 = jnp.sum(acc[...], axis=2, keepdims=True) + p_new * vn_ref[...].reshape(HEADS, HEAD_DIM, 1)
    o_ref[...] = (o3 / denom).reshape(MIX_W, 1)


def _fox_decode(page_table, q_col, kn_col, vn_col, lfn, k_t, v_t, floc, l, nbuf):
    nb, n_pages = page_table.shape
    page = k_t.shape[-1]
    assert nbuf <= n_pages and nbuf & (nbuf - 1) == 0 and n_pages % nbuf == 0
    col = pl.BlockSpec((None, MIX_W, 1), lambda i, pt: (i, 0, 0))
    hbm = pl.BlockSpec(memory_space=pl.ANY)
    grid_spec = pltpu.PrefetchScalarGridSpec(
        num_scalar_prefetch=1,
        grid=(nb,),
        in_specs=[col, col, col, pl.BlockSpec((None, HEADS, 1, 1), lambda i, pt: (i, 0, 0, 0)), hbm, hbm, hbm],
        out_specs=col,
        scratch_shapes=[pltpu.VMEM((nbuf, HEADS, HEAD_DIM, page), F32),
                        pltpu.VMEM((nbuf, HEADS, 1, page), F32),
                        pltpu.SemaphoreType.DMA((nbuf,)),
                        pltpu.SemaphoreType.DMA((nbuf,)),
                        pltpu.VMEM((HEADS, 1, n_pages * page), F32),
                        pltpu.VMEM((HEADS, HEAD_DIM, page), F32),
                        pltpu.VMEM((HEADS, HEAD_DIM, page), F32)],
    )
    return pl.pallas_call(
        functools.partial(_fox_decode_kernel, l=l, nbuf=nbuf),
        out_shape=jax.ShapeDtypeStruct((nb, MIX_W, 1), F32),
        grid_spec=grid_spec,
        compiler_params=_cparams(("arbitrary",)),
        name="fox_decode",
    )(page_table, q_col, kn_col, vn_col, lfn, k_t, v_t, floc)


def _mlstm_decode_kernel(q_ref, k_ref, v_ref, og_ref, li_ref, lf_ref, c_ref, n_ref, m_ref, nw_ref,
                         o_ref, c_out, n_out, m_out):
    q = q_ref[...]
    k = k_ref[...] * (HEAD_DIM ** -0.5)
    v = v_ref[...]
    c = c_ref[...]
    n = n_ref[...]
    m = m_ref[...]
    li = li_ref[...]
    lf = lf_ref[...]
    inter = lf + m
    m_t = jnp.maximum(inter, li)
    a_int = jnp.exp(inter - m_t)
    w_in = jnp.exp(li - m_t)
    s = jnp.sum(q * k, axis=1, keepdims=True) * w_in
    num = a_int * jnp.sum(q * c, axis=1, keepdims=True) + s * v
    den = a_int * jnp.sum(q * n, axis=1, keepdims=True) + s
    h = num / jnp.maximum(jnp.abs(den), jnp.exp(-m_t))
    h = h * lax.rsqrt(jnp.mean(h * h, axis=-1, keepdims=True) + EPS)
    o_ref[...] = _sigmoid(og_ref[...]) * (h * nw_ref[...])
    c_out[...] = a_int * c + w_in * (k * v)
    n_out[...] = a_int * n + w_in * k
    m_out[...] = m_t


def _mlstm_decode(q_col, k_col, v_row, og_row, li, lf, c0, n0_col, m0, nw_row):
    nb = q_col.shape[0]
    d = HEAD_DIM
    colspec = pl.BlockSpec((None, HEADS, d, 1), lambda i: (i, 0, 0, 0))
    rowspec = pl.BlockSpec((None, HEADS, 1, d), lambda i: (i, 0, 0, 0))
    sclspec = pl.BlockSpec((None, HEADS, 1, 1), lambda i: (i, 0, 0, 0))
    matspec = pl.BlockSpec((None, HEADS, d, d), lambda i: (i, 0, 0, 0))
    return pl.pallas_call(
        _mlstm_decode_kernel,
        out_shape=(jax.ShapeDtypeStruct((nb, HEADS, 1, d), F32),
                   jax.ShapeDtypeStruct((nb, HEADS, d, d), F32),
                   jax.ShapeDtypeStruct((nb, HEADS, d, 1), F32),
                   jax.ShapeDtypeStruct((nb, HEADS, 1, 1), F32)),
        grid=(nb,),
        in_specs=[colspec, colspec, rowspec, rowspec, sclspec, sclspec, matspec, colspec, sclspec,
                  pl.BlockSpec((HEADS, 1, d), lambda i: (0, 0, 0))],
        out_specs=(rowspec, matspec, colspec, sclspec),
        compiler_params=_cparams(("arbitrary",)),
        name="mlstm_decode",
    )(q_col, k_col, v_row, og_row, li, lf, c0, n0_col, m0, nw_row)


def _pool_decode_kernel(buf_ref, u_ref, wbd_ref, ps_ref, o_ref, nb_ref, *, pos):
    u = u_ref[...]
    n = buf_ref.shape[0]
    lane = lax.broadcasted_iota(jnp.int32, u.shape, 1)
    sums = {}
    run = u
    for back in range(1, max(POOL_WINDOWS)):
        run = run + buf_ref[n - back]
        if back + 1 in POOL_WINDOWS:
            sums[back + 1] = run
    cnt = {w: float(min(pos + 1, w)) for w in POOL_WINDOWS}
    mean = jnp.where(lane < 64, sums[2] / cnt[2], jnp.where(lane < 128, sums[4] / cnt[4],
                     jnp.where(lane < 192, sums[8] / cnt[8], sums[16] / cnt[16])))
    o_ref[...] = _mm(mean - u, wbd_ref[...]) * ps_ref[...]
    for r in range(n - 1):
        nb_ref[r] = buf_ref[r + 1]
    nb_ref[n - 1] = u


def _pool_decode(buf_t, u, wbd, pscale, l, pos):
    _, n, nb, w = buf_t.shape
    return pl.pallas_call(
        functools.partial(_pool_decode_kernel, pos=pos),
        out_shape=(jax.ShapeDtypeStruct((nb, w), F32), jax.ShapeDtypeStruct((n, nb, w), F32)),
        grid=(1,),
        in_specs=[pl.BlockSpec((None, n, nb, w), lambda i: (l, 0, 0, 0)),
                  pl.BlockSpec((nb, w), lambda i: (0, 0)),
                  pl.BlockSpec((None, w, w), lambda i: (l, 0, 0)),
                  pl.BlockSpec((None, 1, w), lambda i: (l, 0, 0))],
        out_specs=(pl.BlockSpec((nb, w), lambda i: (0, 0)),
                   pl.BlockSpec((n, nb, w), lambda i: (0, 0, 0))),
        compiler_params=_cparams(("arbitrary",)),
        name="pool_decode",
    )(buf_t, u, wbd, pscale)


def _tile(n, pref):
    t = min(n, pref)
    assert n % t == 0, (n, t)
    return t


def kernel(x_prompt, x_sample, c_prompt, c_sample, cache_fox_k, cache_fox_v, cache_fox_lf, page_table,
           state_mlstm_C, state_mlstm_n, state_mlstm_m, state_pool, w_ada, b_ada, norm1_w, norm2_w,
           w_in, b_fox_f, b_mlstm_i, b_mlstm_f, mlstm_norm_w, w_pool, pool_scale, w_out,
           w_ffn_gate, w_ffn_up, w_ffn_down, w_router, w_exp_gate, w_exp_up, w_exp_down, final_norm_w):
    depth = w_in.shape[0]
    b, s, d = x_prompt.shape
    nb = x_sample.shape[0]
    n_pool, page = cache_fox_k.shape[1], cache_fox_k.shape[2]
    assert x_sample.shape[1] == 1 and s % MLSTM_CHUNK == 0 and n_pool % 8 == 0

    pad_gate = lambda w: jnp.pad(w, ((0, 0), (0, 0), (0, LANES - HEADS)))
    w_in_f = jnp.concatenate(
        [w_in[:, :, :_MAIN_W], pad_gate(w_in[:, :, _MAIN_W:_MAIN_W + HEADS]),
         pad_gate(w_in[:, :, _MAIN_W + HEADS:_MAIN_W + 2 * HEADS]),
         pad_gate(w_in[:, :, _MAIN_W + 2 * HEADS:])], axis=-1)
    w_in_p = w_in_f.astype(BF16)
    gate_b = jnp.pad(jnp.stack([b_fox_f, b_mlstm_i, b_mlstm_f], axis=1), ((0, 0), (0, 0), (0, LANES - HEADS)))
    wbd_f = jnp.zeros((depth, POOL_W, POOL_W), F32)
    for g in range(len(POOL_WINDOWS)):
        sl = slice(g * HEAD_DIM, (g + 1) * HEAD_DIM)
        wbd_f = wbd_f.at[:, sl, sl].set(w_pool[:, g])
    wbd = wbd_f.astype(BF16)
    pscale = pool_scale.reshape(depth, 1, POOL_W)
    w_out_b = w_out.astype(BF16)
    nw1 = norm1_w.reshape(depth, 1, d)
    nw2 = norm2_w.reshape(depth, 1, d)
    mnw = mlstm_norm_w.reshape(depth, 1, MIX_W)
    mnw_row = mlstm_norm_w.reshape(depth, HEADS, 1, HEAD_DIM)
    fw = final_norm_w.reshape(1, d)
    w_router_p = jnp.pad(w_router, ((0, 0), (0, 0), (0, LANES - N_EXPERTS)))
    wfg, wfu, wfd = (w.astype(BF16) for w in (w_ffn_gate, w_ffn_up, w_ffn_down))
    weg, weu, wed = (w.astype(BF16) for w in (w_exp_gate, w_exp_up, w_exp_down))

    k_t = jnp.transpose(cache_fox_k, (0, 1, 3, 4, 2))
    v_t = jnp.transpose(cache_fox_v, (0, 1, 3, 4, 2))
    lf_t = jnp.transpose(cache_fox_lf, (0, 3, 1, 2))
    floc = _page_cumsum(lf_t).reshape(depth, HEADS, n_pool // 8, 8, page)
    pool_t = jnp.transpose(state_pool, (0, 2, 1, 3))

    mod = _ada_modulation(jnp.concatenate([c_prompt, c_sample], axis=0), w_ada, b_ada)
    mod_p = mod[:, :, :b].reshape(depth, 6, b, 1, d)
    mod_s = mod[:, :, b:]

    tm = _tile(s, 512)
    tq = _tile(s, 1024)
    tf = w_ffn_gate.shape[-1] // 2
    xp = x_prompt
    xs = x_sample.reshape(1, nb, d)
    st_p, st_s = [], []
    for l in range(depth):
        last = l == depth - 1
        (qa, ka, va, kf, vf, lff, qm, km, vm, om, li, lfm, opool, ulast) = _in_proj_prompt(
            xp, mod_p, l, nw1, w_in_p, gate_b, wbd, pscale, tm)
        o_fox = _fox_prompt(qa, ka, va, tq)
        o_m, c_pair, n_pair, m_fin = _mlstm_prompt(qm, km, vm, om, li, lfm, mnw, l, 2 if b % 2 == 0 else 1)
        x1 = _out_proj(o_fox, o_m, opool, xp, mod_p, l, w_out_b, tm, False)
        if l % 2 == 0:
            j = l // 2
            xp = _ffn(x1, mod_p, l, nw2, wfg[j], wfu[j], wfd[j], fw, tm, tf, False, last)
        else:
            j = l // 2
            xp = _moe(x1, mod_p, l, nw2, w_router_p[j], weg[j], weu[j], wed[j], fw, tm, 256, False, last)
        c_fin = jnp.stack([c_pair[:, p, e * HEAD_DIM:(e + 1) * HEAD_DIM, e * HEAD_DIM:(e + 1) * HEAD_DIM]
                           for p in range(HEADS // 2) for e in range(2)], axis=1)
        st_p.append((kf.reshape(b, s, HEADS, HEAD_DIM), vf.reshape(b, s, HEADS, HEAD_DIM), lff[:, :, :HEADS],
                     c_fin, n_pair.reshape(b, HEADS, HEAD_DIM), m_fin[:, 0, :HEADS],
                     ulast[:, POOL_HALO - POOL_BUF:]))

        z, lff_s, li_s, lfm_s = _in_proj_sample(xs[0], mod_s, l, nw1, w_in_f, gate_b)
        seg = lambda i: z[:, i * MIX_W:(i + 1) * MIX_W]
        qf_s, kf_s, vf_s, qm_s, km_s, vm_s, om_s = (seg(i) for i in range(7))
        u_s = z[:, 7 * MIX_W:]
        col = lambda a: a.reshape(nb, MIX_W, 1)
        o_fox_s = _fox_decode(page_table, col(qf_s), col(kf_s), col(vf_s),
                              lff_s[:, :HEADS].reshape(nb, HEADS, 1, 1), k_t, v_t, floc, l,
                              min(32, page_table.shape[1]))
        hcol = lambda a: a.reshape(nb, HEADS, HEAD_DIM, 1)
        hrow = lambda a: a.reshape(nb, HEADS, 1, HEAD_DIM)
        hscl = lambda a: a[:, :HEADS].reshape(nb, HEADS, 1, 1)
        o_m_s, c_new, n_new, m_new = _mlstm_decode(
            hcol(qm_s), hcol(km_s), hrow(vm_s), hrow(om_s), hscl(li_s), hscl(lfm_s),
            state_mlstm_C[l], hcol(state_mlstm_n[l]), state_mlstm_m[l].reshape(nb, HEADS, 1, 1), mnw_row[l])
        o_pool_s, buf_new = _pool_decode(pool_t, u_s, wbd_f, pscale, l, page_table.shape[1] * page)
        mix = lambda a: a.reshape(1, nb, -1)
        x1s = _out_proj(mix(o_fox_s), mix(o_m_s), mix(o_pool_s), xs, mod_s, l, w_out, nb, True)
        if l % 2 == 0:
            xs = _ffn(x1s, mod_s, l, nw2, w_ffn_gate[j], w_ffn_up[j], w_ffn_down[j], fw, nb, 2 * LANES, True, last)
        else:
            xs = _moe(x1s, mod_s, l, nw2, w_router_p[j], weg[j], weu[j], wed[j], fw, nb, 8, True, last)
        st_s.append((kf_s.reshape(nb, 1, HEADS, HEAD_DIM), vf_s.reshape(nb, 1, HEADS, HEAD_DIM),
                     lff_s[:, :HEADS].reshape(nb, 1, HEADS), c_new, n_new.reshape(nb, HEADS, HEAD_DIM),
                     m_new.reshape(nb, HEADS), jnp.transpose(buf_new, (1, 0, 2))))

    outs_p = [jnp.stack(a) for a in zip(*st_p)]
    outs_s = [jnp.stack(a) for a in zip(*st_s)]
    return (xp, xs.reshape(nb, 1, d), *outs_p, *outs_s)
```

```python
import functools

import jax
import jax.numpy as jnp
from jax import lax
from jax.experimental import pallas as pl
from jax.experimental.pallas import tpu as pltpu

F32 = jnp.float32
BF16 = jnp.bfloat16

HEAD_DIM = 64
LANES = 128
MXU_TILE = 256
HEADS = 6
MIX_W = HEADS * HEAD_DIM
POOL_WINDOWS = (2, 4, 8, 16)
POOL_W = 256
POOL_HALO = 16
POOL_BUF = 15
MLSTM_CHUNK = 128
N_EXPERTS = 8
EPS = 1e-6
NEG_INF = float("-inf")

_MAIN_W = 7 * MIX_W + POOL_W
_GATE_OFF = _MAIN_W
IN_PAD_W = _MAIN_W + 3 * LANES

VMEM_LIMIT = 56 * 1024 * 1024


def _cparams(sem):
    return pltpu.CompilerParams(dimension_semantics=sem, vmem_limit_bytes=VMEM_LIMIT)


def _sigmoid(x):
    return 1.0 / (1.0 + jnp.exp(-x))


def _log_sigmoid(x):
    return jnp.minimum(x, 0.0) - jnp.log(1.0 + jnp.exp(-jnp.abs(x)))


def _split3(x):
    hi = x.astype(BF16).astype(F32)
    r = x - hi
    mid = r.astype(BF16).astype(F32)
    lo = (r - mid).astype(BF16).astype(F32)
    return hi, mid, lo


def _dot(a, b):
    return jnp.dot(a, b, preferred_element_type=F32)


def _dot_nt(a, b):
    return lax.dot_general(a, b, (((1,), (1,)), ((), ())), preferred_element_type=F32)


def _dot_tn(a, b):
    return lax.dot_general(a, b, (((0,), (0,)), ((), ())), preferred_element_type=F32)


def _tri_dot(tri_bf16, x):
    hi, mid, lo = _split3(x)
    return (_dot(tri_bf16, hi.astype(BF16)) + _dot(tri_bf16, mid.astype(BF16))
            + _dot(tri_bf16, lo.astype(BF16)))


def _dot_tri(x, tri_bf16):
    hi, mid, lo = _split3(x)
    return (_dot(hi.astype(BF16), tri_bf16) + _dot(mid.astype(BF16), tri_bf16)
            + _dot(lo.astype(BF16), tri_bf16))


def _dot_f32(a, b, passes=6):
    a0, a1, a2 = (p.astype(BF16) for p in _split3(a))
    b0, b1, b2 = (p.astype(BF16) for p in _split3(b))
    out = _dot(a0, b1) + _dot(a1, b0)
    if passes == 6:
        out = (_dot(a0, b2) + _dot(a2, b0) + _dot(a1, b1)) + out
    return out + _dot(a0, b0)


def _mm(a, w):
    if w.dtype == F32:
        return _dot_f32(a.astype(F32), w)
    return _dot(a.astype(BF16), w)


def _lower_tri(n):
    r = lax.broadcasted_iota(jnp.int32, (n, n), 0)
    c = lax.broadcasted_iota(jnp.int32, (n, n), 1)
    return r >= c


def _modulated_rmsnorm(x, nw, sc, sh):
    y = x * lax.rsqrt(jnp.mean(x * x, axis=-1, keepdims=True) + EPS)
    return (y * nw) * (1.0 + sc) + sh


def _ada_kernel(c_ref, w_ref, b_ref, o_ref):
    c = c_ref[...]
    s = c * _sigmoid(c)
    o_ref[...] = _mm(s, w_ref[...]) + b_ref[...]


def _ada_modulation(c_all, w_ada, b_ada):
    depth, d, _ = w_ada.shape
    r = c_all.shape[0]
    return pl.pallas_call(
        _ada_kernel,
        out_shape=jax.ShapeDtypeStruct((depth, 6, r, d), F32),
        grid=(depth, 6),
        in_specs=[
            pl.BlockSpec((r, d), lambda l, j: (0, 0)),
            pl.BlockSpec((None, d, d), lambda l, j: (l, 0, j)),
            pl.BlockSpec((None, 1, d), lambda l, j: (l, 0, j)),
        ],
        out_specs=pl.BlockSpec((None, None, r, d), lambda l, j: (l, j, 0, 0)),
        compiler_params=_cparams(("arbitrary", "arbitrary")),
        name="ada_modulation",
    )(c_all, w_ada, b_ada.reshape(depth, 1, 6 * d))


def _pool_mix(u, halo, pos0, wbd, scale):
    t = u.shape[0]
    ext = jnp.concatenate([halo, u], axis=0)
    p2 = ext + pltpu.roll(ext, 1, 0)
    p4 = p2 + pltpu.roll(p2, 2, 0)
    p8 = p4 + pltpu.roll(p4, 4, 0)
    p16 = p8 + pltpu.roll(p8, 8, 0)
    lane = lax.broadcasted_iota(jnp.int32, (t, POOL_W), 1)
    sums = jnp.where(lane < 64, p2[POOL_HALO:], jnp.where(lane < 128, p4[POOL_HALO:],
                     jnp.where(lane < 192, p8[POOL_HALO:], p16[POOL_HALO:])))
    win = jnp.where(lane < 64, 2.0, jnp.where(lane < 128, 4.0, jnp.where(lane < 192, 8.0, 16.0)))
    pos = (pos0 + lax.broadcasted_iota(jnp.int32, (t, POOL_W), 0)).astype(F32)
    cnt = jnp.minimum(pos + 1.0, win)
    d = sums / cnt - u
    return _dot(d.astype(BF16), wbd) * scale


def _in_proj_prompt_kernel(x_ref, nw_ref, sc_ref, sh_ref, w_ref, gb_ref, wbd_ref, ps_ref,
                           qa_ref, ka_ref, va_ref, kf_ref, vf_ref, lff_ref,
                           qm_ref, km_ref, vm_ref, om_ref, li_ref, lfm_ref, op_ref, ul_ref,
                           fcarry, uhalo):
    t = pl.program_id(1)
    tm = x_ref.shape[0]

    @pl.when(t == 0)
    def _():
        fcarry[...] = jnp.zeros_like(fcarry)
        uhalo[...] = jnp.zeros_like(uhalo)

    hb = _modulated_rmsnorm(x_ref[...], nw_ref[...], sc_ref[...], sh_ref[...]).astype(BF16)
    split = 4 * MIX_W
    za = _dot(hb, w_ref[:, 0:split])
    zb = _dot(hb, w_ref[:, split:IN_PAD_W])
    piece_a = lambda i: za[:, i * MIX_W:(i + 1) * MIX_W]
    piece_b = lambda i: zb[:, i * MIX_W:(i + 1) * MIX_W]

    lane = lax.broadcasted_iota(jnp.int32, (tm, LANES), 1)
    zg = zb[:, _GATE_OFF - split:IN_PAD_W - split]
    head_lane = lane < HEADS
    lf_fox = jnp.where(head_lane, _log_sigmoid(zg[:, 0:LANES] + gb_ref[0:1, :]), 0.0)
    li = jnp.where(head_lane, zg[:, LANES:2 * LANES] + gb_ref[1:2, :], 0.0)
    lf_m = jnp.where(head_lane, _log_sigmoid(zg[:, 2 * LANES:3 * LANES] + gb_ref[2:3, :]), 0.0)
    lff_ref[...] = lf_fox
    li_ref[...] = li
    lfm_ref[...] = lf_m

    cum = _tri_dot(_lower_tri(tm).astype(BF16), lf_fox) + fcarry[...]
    fcarry[...] = cum[tm - 1:tm, :]

    zq = piece_a(0) * (HEAD_DIM ** -0.5)
    zk = piece_a(1)
    zv = piece_a(2)
    kf_ref[...] = zk
    vf_ref[...] = zv
    low = lane < HEAD_DIM
    for h in range(HEADS):
        p, e = divmod(h, 2)
        sl = slice(p * LANES, (p + 1) * LANES)
        bq, bk, bv = zq[:, sl], zk[:, sl], zv[:, sl]
        if e == 1:
            bq, bk, bv = (pltpu.roll(a, HEAD_DIM, 1) for a in (bq, bk, bv))
        hi, mid, lo = _split3(cum[:, h:h + 1])
        augq = jnp.where(lane == 64, hi, jnp.where(lane == 65, mid, jnp.where(lane == 66, lo,
                         jnp.where(lane < 70, 1.0, 0.0))))
        augk = jnp.where(lane < 67, 1.0, jnp.where(lane == 67, -hi, jnp.where(lane == 68, -mid,
                         jnp.where(lane == 69, -lo, 0.0))))
        augv = jnp.where(lane == 64, 1.0, 0.0)
        qa_ref[h] = jnp.where(low, bq, augq).astype(BF16)
        ka_ref[h] = jnp.where(low, bk, augk).astype(BF16)
        va_ref[h] = jnp.where(low, bv, augv).astype(BF16)

    qm_ref[...] = piece_a(3).astype(BF16)
    km_ref[...] = (piece_b(0) * (HEAD_DIM ** -0.5)).astype(BF16)
    vm_ref[...] = piece_b(1).astype(BF16)
    om_ref[...] = piece_b(2)

    u = zb[:, 3 * MIX_W:3 * MIX_W + POOL_W]
    op_ref[...] = _pool_mix(u, uhalo[...], t * tm, wbd_ref[...], ps_ref[...]).astype(BF16)
    uhalo[...] = u[tm - POOL_HALO:, :]
    ul_ref[...] = u[tm - POOL_HALO:, :]


def _in_proj_prompt(x, mod, l, nw, w_in_p, gate_b, wbd, pscale, tm):
    b, s, d = x.shape
    nt = s // tm
    row = lambda width, dt: jax.ShapeDtypeStruct((b, s, width), dt)
    head = jax.ShapeDtypeStruct((b, HEADS, s, LANES), BF16)
    tok = lambda width: pl.BlockSpec((None, tm, width), lambda i, t: (i, t, 0))
    hspec = pl.BlockSpec((None, HEADS, tm, LANES), lambda i, t: (i, 0, t, 0))
    modspec = lambda j: pl.BlockSpec((None, None, None, 1, d), lambda i, t: (l, j, i, 0, 0))
    return pl.pallas_call(
        _in_proj_prompt_kernel,
        out_shape=(head, head, head, row(MIX_W, F32), row(MIX_W, F32), row(LANES, F32),
                   row(MIX_W, BF16), row(MIX_W, BF16), row(MIX_W, BF16), row(MIX_W, F32),
                   row(LANES, F32), row(LANES, F32), row(POOL_W, BF16),
                   jax.ShapeDtypeStruct((b, POOL_HALO, POOL_W), F32)),
        grid=(b, nt),
        in_specs=[
            tok(d),
            pl.BlockSpec((None, 1, d), lambda i, t: (l, 0, 0)),
            modspec(1), modspec(0),
            pl.BlockSpec((None, d, IN_PAD_W), lambda i, t: (l, 0, 0)),
            pl.BlockSpec((None, 3, LANES), lambda i, t: (l, 0, 0)),
            pl.BlockSpec((None, POOL_W, POOL_W), lambda i, t: (l, 0, 0)),
            pl.BlockSpec((None, 1, POOL_W), lambda i, t: (l, 0, 0)),
        ],
        out_specs=(hspec, hspec, hspec, tok(MIX_W), tok(MIX_W), tok(LANES),
                   tok(MIX_W), tok(MIX_W), tok(MIX_W), tok(MIX_W), tok(LANES), tok(LANES),
                   tok(POOL_W), pl.BlockSpec((None, POOL_HALO, POOL_W), lambda i, t: (i, 0, 0))),
        scratch_shapes=[pltpu.VMEM((1, LANES), F32), pltpu.VMEM((POOL_HALO, POOL_W), F32)],
        compiler_params=_cparams(("arbitrary", "arbitrary")),
        name="in_proj_prompt",
    )(x, nw, mod, mod, w_in_p, gate_b, wbd, pscale)


def _in_proj_sample_kernel(x_ref, nw_ref, sc_ref, sh_ref, w_ref, gb_ref, z_ref, lff_ref, li_ref, lfm_ref):
    h = _modulated_rmsnorm(x_ref[...], nw_ref[...], sc_ref[...], sh_ref[...])
    chunk = 4 * LANES
    for c0 in range(0, _MAIN_W, chunk):
        c1 = min(c0 + chunk, _MAIN_W)
        z_ref[:, c0:c1] = _mm(h, w_ref[:, c0:c1])
    zg = _mm(h, w_ref[:, _GATE_OFF:IN_PAD_W])
    lff_ref[...] = _log_sigmoid(zg[:, 0:LANES] + gb_ref[0:1, :])
    li_ref[...] = zg[:, LANES:2 * LANES] + gb_ref[1:2, :]
    lfm_ref[...] = _log_sigmoid(zg[:, 2 * LANES:3 * LANES] + gb_ref[2:3, :])


def _in_proj_sample(x, mod, l, nw, w_in_p, gate_b):
    n, d = x.shape
    gate = jax.ShapeDtypeStruct((n, LANES), F32)
    full = lambda *shape: pl.BlockSpec(shape, lambda i: (0,) * len(shape))
    modspec = lambda j: pl.BlockSpec((None, None, n, d), lambda i: (l, j, 0, 0))
    return pl.pallas_call(
        _in_proj_sample_kernel,
        out_shape=(jax.ShapeDtypeStruct((n, _MAIN_W), F32), gate, gate, gate),
        grid=(1,),
        in_specs=[
            full(n, d),
            pl.BlockSpec((None, 1, d), lambda i: (l, 0, 0)),
            modspec(1), modspec(0),
            pl.BlockSpec((None, d, IN_PAD_W), lambda i: (l, 0, 0)),
            pl.BlockSpec((None, 3, LANES), lambda i: (l, 0, 0)),
        ],
        out_specs=(full(n, _MAIN_W), full(n, LANES), full(n, LANES), full(n, LANES)),
        compiler_params=_cparams(("arbitrary",)),
        name="in_proj_sample",
    )(x, nw, mod, mod, w_in_p, gate_b)


def _fox_prompt_kernel(qa_ref, ka_ref, va_ref, o_ref, *, tq):
    qi = pl.program_id(2)
    q = (qa_ref[0], qa_ref[1])

    def block(e, ki, carry, diag):
        m, acc = carry
        k = ka_ref[e, pl.ds(pl.multiple_of(ki * tq, tq), tq), :]
        v = va_ref[e, pl.ds(pl.multiple_of(ki * tq, tq), tq), :]
        s = _dot_nt(q[e], k)
        if diag:
            s = jnp.where(_lower_tri(tq), s, NEG_INF)
        m_new = jnp.maximum(m, jnp.max(s, axis=-1, keepdims=True))
        p = jnp.exp(s - m_new)
        acc = acc * jnp.exp(m - m_new) + _dot(p.astype(BF16), v)
        return m_new, acc

    def body(ki, carry):
        return block(0, ki, carry[0], False), block(1, ki, carry[1], False)

    init = (jnp.full((tq, 1), NEG_INF, F32), jnp.zeros((tq, LANES), F32))
    carry = lax.fori_loop(0, qi, body, (init, init))
    outs = []
    for e in range(2):
        _, acc = block(e, qi, carry[e], True)
        outs.append(acc / acc[:, HEAD_DIM:HEAD_DIM + 1])
    lane = lax.broadcasted_iota(jnp.int32, (tq, LANES), 1)
    o_ref[...] = jnp.where(lane < HEAD_DIM, outs[0], pltpu.roll(outs[1], HEAD_DIM, 1)).astype(o_ref.dtype)


def _fox_prompt(qa, ka, va, tq):
    b, _, s, _ = qa.shape
    return pl.pallas_call(
        functools.partial(_fox_prompt_kernel, tq=tq),
        out_shape=jax.ShapeDtypeStruct((b, s, MIX_W), BF16),
        grid=(b, HEADS // 2, s // tq),
        in_specs=[
            pl.BlockSpec((None, 2, tq, LANES), lambda i, p, t: (i, p, t, 0)),
            pl.BlockSpec((None, 2, s, LANES), lambda i, p, t: (i, p, 0, 0)),
            pl.BlockSpec((None, 2, s, LANES), lambda i, p, t: (i, p, 0, 0)),
        ],
        out_specs=pl.BlockSpec((None, tq, LANES), lambda i, p, t: (i, t, p)),
        compiler_params=_cparams(("arbitrary", "arbitrary", "arbitrary")),
        name="fox_prompt",
    )(qa, ka, va)


def _mlstm_prompt_kernel(q_ref, k_ref, v_ref, og_ref, li_ref, lf_ref, nw_ref,
                         o_ref, c_out, n_out, m_out, c_st, n_st, m_st):
    c = pl.program_id(1)
    nb, L, _ = q_ref.shape

    @pl.when(c == 0)
    def _():
        c_st[...] = jnp.zeros_like(c_st)
        n_st[...] = jnp.zeros_like(n_st)
        m_st[...] = jnp.zeros_like(m_st)

    tril = _lower_tri(L)
    tri = tril.astype(BF16)
    lane = lax.broadcasted_iota(jnp.int32, (L, LANES), 1)
    lane1 = lax.broadcasted_iota(jnp.int32, (1, LANES), 1)
    rowi = lax.broadcasted_iota(jnp.int32, (LANES, LANES), 0)
    coli = lax.broadcasted_iota(jnp.int32, (LANES, LANES), 1)
    same_head = (rowi < HEAD_DIM) == (coli < HEAD_DIM)

    for bi in range(nb):
        lf = lf_ref[bi]
        li = li_ref[bi]
        bcum = _tri_dot(tri, lf)
        m_prev = m_st[bi]
        b_end = bcum[L - 1:L, :]
        g = b_end - bcum + li
        m_new = jnp.maximum(b_end + m_prev, jnp.max(g, axis=0, keepdims=True))
        a_state = jnp.exp(b_end + m_prev - m_new)
        wg = jnp.exp(g - m_new)
        inter = bcum + m_prev
        bcum_t = bcum.T
        li_t = li.T
        heads = range(HEADS)
        pairs = [slice(p * LANES, (p + 1) * LANES) for p in range(HEADS // 2)]
        qs = [q_ref[bi, :, sl] for sl in pairs]
        ks = [k_ref[bi, :, sl] for sl in pairs]
        vs = [v_ref[bi, :, sl] for sl in pairs]
        mine = [(lane < HEAD_DIM) if h % 2 == 0 else (lane >= HEAD_DIM) for h in heads]
        own = jnp.stack(mine)
        q6 = jnp.stack([jnp.where(mine[h], qs[h // 2], jnp.zeros_like(qs[0])) for h in heads])
        k6 = jnp.stack([ks[h // 2] for h in heads])
        v6 = jnp.stack([vs[h // 2] for h in heads])
        n6 = jnp.stack([n_st[bi, h // 2] for h in heads])
        qc3 = [_dot(qs[p], c_st[bi, p].astype(BF16)) for p in range(HEADS // 2)]
        qc6 = jnp.stack([qc3[h // 2] for h in heads])
        bcol = jnp.stack([bcum[:, h:h + 1] for h in heads])
        icol = jnp.stack([inter[:, h:h + 1] for h in heads])
        brow = jnp.stack([bcum_t[h:h + 1, :] for h in heads])
        lrow = jnp.stack([li_t[h:h + 1, :] for h in heads])
        dmat = jnp.where(tril[None], bcol - brow + lrow, NEG_INF)
        m_t = jnp.maximum(icol, jnp.max(dmat, axis=-1, keepdims=True))
        a_int = jnp.exp(icol - m_t)
        s = lax.dot_general(q6, k6, (((2,), (2,)), ((0,), (0,))), preferred_element_type=F32) \
            * jnp.exp(dmat - m_t)
        sv = lax.dot_general(s.astype(BF16), v6, (((2,), (1,)), ((0,), (0,))), preferred_element_type=F32)
        den = a_int * jnp.sum(q6.astype(F32) * n6, axis=-1, keepdims=True) + jnp.sum(s, axis=-1, keepdims=True)
        hh = (a_int * qc6 + sv) / jnp.maximum(jnp.abs(den), jnp.exp(-m_t))
        ms = jnp.sum(jnp.where(own, hh * hh, 0.0), axis=-1, keepdims=True) * (1.0 / HEAD_DIM)
        hn = hh * lax.rsqrt(ms + EPS)
        for p in range(HEADS // 2):
            sl = pairs[p]
            k, v = ks[p], vs[p]
            cp = c_st[bi, p]
            npair = n_st[bi, p]
            hpair = jnp.where(lane < HEAD_DIM, hn[2 * p], hn[2 * p + 1])
            o_ref[bi, :, sl] = (_sigmoid(og_ref[bi, :, sl]) * (hpair * nw_ref[:, sl])).astype(o_ref.dtype)
            wgp = jnp.where(lane < HEAD_DIM, wg[:, 2 * p:2 * p + 1], wg[:, 2 * p + 1:2 * p + 2])
            ap = jnp.where(lane1 < HEAD_DIM, a_state[:, 2 * p:2 * p + 1], a_state[:, 2 * p + 1:2 * p + 2])
            kv = _dot_tn(k, (v.astype(F32) * wgp).astype(BF16))
            c_st[bi, p] = cp * ap + jnp.where(same_head, kv, 0.0)
            n_st[bi, p] = npair * ap + jnp.sum(k.astype(F32) * wgp, axis=0, keepdims=True)
        m_st[bi] = m_new

    @pl.when(c == pl.num_programs(1) - 1)
    def _():
        c_out[...] = c_st[...]
        n_out[...] = n_st[...]
        m_out[...] = m_st[...]


def _mlstm_prompt(qm, km, vm, om, li, lfm, nw, l, nb):
    b, s, _ = qm.shape
    L = MLSTM_CHUNK
    np_ = HEADS // 2
    tok = lambda w: pl.BlockSpec((nb, L, w), lambda i, c: (i, c, 0))
    return pl.pallas_call(
        _mlstm_prompt_kernel,
        out_shape=(jax.ShapeDtypeStruct((b, s, MIX_W), BF16),
                   jax.ShapeDtypeStruct((b, np_, LANES, LANES), F32),
                   jax.ShapeDtypeStruct((b, np_, 1, LANES), F32),
                   jax.ShapeDtypeStruct((b, 1, LANES), F32)),
        grid=(b // nb, s // L),
        in_specs=[tok(MIX_W), tok(MIX_W), tok(MIX_W), tok(MIX_W), tok(LANES), tok(LANES),
                  pl.BlockSpec((None, 1, MIX_W), lambda i, c: (l, 0, 0))],
        out_specs=(tok(MIX_W),
                   pl.BlockSpec((nb, np_, LANES, LANES), lambda i, c: (i, 0, 0, 0)),
                   pl.BlockSpec((nb, np_, 1, LANES), lambda i, c: (i, 0, 0, 0)),
                   pl.BlockSpec((nb, 1, LANES), lambda i, c: (i, 0, 0))),
        scratch_shapes=[pltpu.VMEM((nb, np_, LANES, LANES), F32),
                        pltpu.VMEM((nb, np_, 1, LANES), F32),
                        pltpu.VMEM((nb, 1, LANES), F32)],
        compiler_params=_cparams(("arbitrary", "arbitrary")),
        name="mlstm_prompt",
    )(qm, km, vm, om, li, lfm, nw)


def _out_proj_kernel(of_ref, om_ref, op_ref, x_ref, g_ref, w_ref, o_ref):
    y = _mm(jnp.concatenate([of_ref[...], om_ref[...], op_ref[...]], axis=1), w_ref[...])
    o_ref[...] = x_ref[...] + g_ref[...] * y


def _out_proj(of, om, op, x, mod, l, w_out_b, tm, per_row_mod):
    b, s, d = x.shape
    tok = lambda w: pl.BlockSpec((None, tm, w), lambda i, t: (i, t, 0))
    if per_row_mod:
        gspec = pl.BlockSpec((None, None, tm, d), lambda i, t: (l, 2, 0, 0))
    else:
        gspec = pl.BlockSpec((None, None, None, 1, d), lambda i, t: (l, 2, i, 0, 0))
    return pl.pallas_call(
        _out_proj_kernel,
        out_shape=jax.ShapeDtypeStruct((b, s, d), F32),
        grid=(b, s // tm),
        in_specs=[tok(MIX_W), tok(MIX_W), tok(POOL_W), tok(d), gspec,
                  pl.BlockSpec((None, d, d), lambda i, t: (l, 0, 0))],
        out_specs=tok(d),
        compiler_params=_cparams(("arbitrary", "arbitrary")),
        name="out_proj",
    )(of, om, op, x, mod, w_out_b)


def _final_norm(x, fw):
    return (x * lax.rsqrt(jnp.mean(x * x, axis=-1, keepdims=True) + EPS)) * fw


def _swiglu_chunks(h, wg_ref, wu_ref, wd_ref, chunk=MXU_TILE):
    ff = wg_ref.shape[-1]
    y = None
    for c0 in range(0, ff, chunk):
        c1 = min(c0 + chunk, ff)
        a = _mm(h, wg_ref[:, c0:c1])
        u = _mm(h, wu_ref[:, c0:c1])
        part = _mm((a * _sigmoid(a)) * u, wd_ref[c0:c1, :])
        y = part if y is None else y + part
    return y


def _ffn_kernel(x_ref, nw_ref, sc_ref, sh_ref, g_ref, wg_ref, wu_ref, wd_ref, fw_ref, o_ref,
                h_sc, acc, *, final):
    j = pl.program_id(2)

    @pl.when(j == 0)
    def _():
        h_sc[...] = _modulated_rmsnorm(x_ref[...], nw_ref[...], sc_ref[...], sh_ref[...]).astype(h_sc.dtype)
        acc[...] = jnp.zeros_like(acc)

    h = h_sc[...]
    acc[...] += _swiglu_chunks(h, wg_ref, wu_ref, wd_ref)

    @pl.when(j == pl.num_programs(2) - 1)
    def _():
        y = x_ref[...] + g_ref[...] * acc[...]
        o_ref[...] = _final_norm(y, fw_ref[...]) if final else y


def _mod_specs(l, d, tm, per_row_mod, nidx):
    def spec(j):
        if per_row_mod:
            return pl.BlockSpec((None, None, tm, d), lambda *a: (l, j, 0, 0))
        return pl.BlockSpec((None, None, None, 1, d), lambda *a: (l, j, a[0], 0, 0))
    return spec(4), spec(3), spec(5)


def _ffn(x, mod, l, nw, wg, wu, wd, fw, tm, tf, per_row_mod, final):
    b, s, d = x.shape
    ff = wg.shape[-1]
    sc, sh, g = _mod_specs(l, d, tm, per_row_mod, 3)
    return pl.pallas_call(
        functools.partial(_ffn_kernel, final=final),
        out_shape=jax.ShapeDtypeStruct((b, s, d), F32),
        grid=(b, s // tm, ff // tf),
        in_specs=[
            pl.BlockSpec((None, tm, d), lambda i, t, j: (i, t, 0)),
            pl.BlockSpec((None, 1, d), lambda i, t, j: (l, 0, 0)),
            sc, sh, g,
            pl.BlockSpec((d, tf), lambda i, t, j: (0, j)),
            pl.BlockSpec((d, tf), lambda i, t, j: (0, j)),
            pl.BlockSpec((tf, d), lambda i, t, j: (j, 0)),
            pl.BlockSpec((1, d), lambda i, t, j: (0, 0)),
        ],
        out_specs=pl.BlockSpec((None, tm, d), lambda i, t, j: (i, t, 0)),
        scratch_shapes=[pltpu.VMEM((tm, d), wg.dtype), pltpu.VMEM((tm, d), F32)],
        compiler_params=_cparams(("arbitrary", "arbitrary", "arbitrary")),
        name="ffn_swiglu",
    )(x, nw, mod, mod, mod, wg, wu, wd, fw)


def _top2(logits):
    lane = lax.broadcasted_iota(jnp.int32, logits.shape, 1)
    valid = lane < N_EXPERTS
    z = jnp.where(valid, logits, NEG_INF)
    pz = jnp.exp(z - jnp.max(z, axis=-1, keepdims=True))
    probs = pz / jnp.sum(pz, axis=-1, keepdims=True)
    m1 = jnp.max(probs, axis=-1, keepdims=True)
    i1 = jnp.min(jnp.where(probs == m1, lane, LANES), axis=-1, keepdims=True)
    rest = jnp.where((lane == i1) | (~valid), -1.0, probs)
    m2 = jnp.max(rest, axis=-1, keepdims=True)
    i2 = jnp.min(jnp.where(rest == m2, lane, LANES), axis=-1, keepdims=True)
    tot = m1 + m2
    return i1, i2, m1 / tot, m2 / tot


_R_E1, _R_E2, _R_R1, _R_R2, _R_G1, _R_G2 = range(6)


def _moe_route_kernel(x_ref, nw_ref, sc_ref, sh_ref, wr_ref, h_ref, route_ref, cnt_ref, carry):
    first = (pl.program_id(0) == 0) & (pl.program_id(1) == 0)

    @pl.when(first)
    def _():
        carry[...] = jnp.zeros_like(carry)

    h = _modulated_rmsnorm(x_ref[...], nw_ref[...], sc_ref[...], sh_ref[...])
    h_ref[...] = h
    tm = h.shape[0]
    i1, i2, g1, g2 = _top2(_dot_f32(h, wr_ref[...], passes=3))
    lane = lax.broadcasted_iota(jnp.int32, (tm, LANES), 1)
    oh1 = lane == i1
    oh2 = lane == i2
    both = jnp.where(oh1 | oh2, 1.0, 0.0)
    r = lax.broadcasted_iota(jnp.int32, (tm, tm), 0)
    c = lax.broadcasted_iota(jnp.int32, (tm, tm), 1)
    before = carry[...] + _dot((r > c).astype(BF16), both.astype(BF16))
    r1 = jnp.sum(jnp.where(oh1, before, 0.0), axis=-1, keepdims=True)
    r2 = jnp.sum(jnp.where(oh2, before, 0.0), axis=-1, keepdims=True)
    carry[...] += jnp.sum(both, axis=0, keepdims=True)
    cnt_ref[...] = carry[...]
    vals = (i1.astype(F32), i2.astype(F32), r1, r2, g1, g2)
    route = jnp.zeros((tm, LANES), F32)
    for k, v in enumerate(vals):
        route = jnp.where(lane == k, v, route)
    route_ref[...] = route


def _moe_route(x, mod, l, nw, wr, tm, per_row_mod):
    b, s, d = x.shape
    sc, sh, _ = _mod_specs(l, d, tm, per_row_mod, 2)
    tok = lambda w: pl.BlockSpec((None, tm, w), lambda i, t: (i, t, 0))
    return pl.pallas_call(
        _moe_route_kernel,
        out_shape=(jax.ShapeDtypeStruct((b, s, d), F32), jax.ShapeDtypeStruct((b, s, LANES), F32),
                   jax.ShapeDtypeStruct((1, LANES), F32)),
        grid=(b, s // tm),
        in_specs=[tok(d), pl.BlockSpec((None, 1, d), lambda i, t: (l, 0, 0)), sc, sh,
                  pl.BlockSpec((d, LANES), lambda i, t: (0, 0))],
        out_specs=(tok(d), tok(LANES), pl.BlockSpec((1, LANES), lambda i, t: (0, 0))),
        scratch_shapes=[pltpu.VMEM((1, LANES), F32)],
        compiler_params=_cparams(("arbitrary", "arbitrary")),
        name="moe_route",
    )(x, nw, mod, mod, wr)


def _row_copy(src, src_row, dst, dst_row, sem):
    return pltpu.make_async_copy(src.at[pl.ds(src_row, 1)], dst.at[pl.ds(dst_row, 1)], sem)


def _moe_dispatch_kernel(zrow_ref, d1_ref, d2_ref, h_ref, hs_ref, zbuf, sem, zsem):
    tm = h_ref.shape[0]
    tg = zbuf.shape[0]

    @pl.when(pl.program_id(0) == 0)
    def _():
        zbuf[...] = jnp.zeros_like(zbuf)
        fills = [pltpu.make_async_copy(
            zbuf, hs_ref.at[pl.ds(pl.multiple_of(jnp.maximum(zrow_ref[e], 0), tg), tg)], zsem)
            for e in range(N_EXPERTS)]
        for e in range(N_EXPERTS):
            @pl.when(zrow_ref[e] >= 0)
            def _():
                fills[e].start()
        for e in range(N_EXPERTS):
            @pl.when(zrow_ref[e] >= 0)
            def _():
                fills[e].wait()

        def tail(t):
            return pltpu.make_async_copy(zbuf, hs_ref.at[pl.ds(pl.multiple_of(t * tg, tg), tg)], zsem)

        n_tiles = hs_ref.shape[0] // tg
        lax.fori_loop(zrow_ref[N_EXPERTS], n_tiles, lambda t, c: (tail(t).start(), c)[1], 0)
        lax.fori_loop(zrow_ref[N_EXPERTS], n_tiles, lambda t, c: (tail(t).wait(), c)[1], 0)

    def start(r, _):
        _row_copy(h_ref, r, hs_ref, d1_ref[0, r], sem).start()
        _row_copy(h_ref, r, hs_ref, d2_ref[0, r], sem).start()
        return 0

    def wait(r, _):
        _row_copy(h_ref, r, hs_ref, d1_ref[0, r], sem).wait()
        _row_copy(h_ref, r, hs_ref, d2_ref[0, r], sem).wait()
        return 0

    lax.fori_loop(0, tm, start, 0, unroll=8)
    lax.fori_loop(0, tm, wait, 0, unroll=8)


def _moe_dispatch(h, dest1, dest2, zrow, rows, tm, tg):
    n, d = h.shape
    idx = pl.BlockSpec((None, 1, tm), lambda i, z: (i, 0, 0), memory_space=pltpu.SMEM)
    grid_spec = pltpu.PrefetchScalarGridSpec(
        num_scalar_prefetch=1,
        grid=(n // tm,),
        in_specs=[idx, idx, pl.BlockSpec((tm, d), lambda i, z: (i, 0))],
        out_specs=pl.BlockSpec(memory_space=pl.ANY),
        scratch_shapes=[pltpu.VMEM((tg, d), F32), pltpu.SemaphoreType.DMA, pltpu.SemaphoreType.DMA],
    )
    return pl.pallas_call(
        _moe_dispatch_kernel,
        out_shape=jax.ShapeDtypeStruct((rows, d), F32),
        grid_spec=grid_spec,
        compiler_params=_cparams(("arbitrary",)),
        name="moe_dispatch",
    )(zrow, dest1, dest2, h)


def _moe_experts_kernel(te_ref, nv_ref, x_ref, wg_ref, wu_ref, wd_ref, o_ref):
    del te_ref
    i = pl.program_id(0)

    @pl.when(i < nv_ref[0])
    def _():
        o_ref[...] = _swiglu_chunks(x_ref[...].astype(BF16), wg_ref, wu_ref, wd_ref, chunk=3 * MXU_TILE)

    @pl.when(i >= nv_ref[0])
    def _():
        o_ref[...] = jnp.zeros_like(o_ref)


def _moe_experts(hs, tile_expert, n_valid, wg, wu, wd, tg):
    rows, d = hs.shape
    ff = wg.shape[-1]
    grid_spec = pltpu.PrefetchScalarGridSpec(
        num_scalar_prefetch=2,
        grid=(rows // tg,),
        in_specs=[pl.BlockSpec((tg, d), lambda i, te, nv: (i, 0)),
                  pl.BlockSpec((None, d, ff), lambda i, te, nv: (te[i], 0, 0)),
                  pl.BlockSpec((None, d, ff), lambda i, te, nv: (te[i], 0, 0)),
                  pl.BlockSpec((None, ff, d), lambda i, te, nv: (te[i], 0, 0))],
        out_specs=pl.BlockSpec((tg, d), lambda i, te, nv: (i, 0)),
    )
    return pl.pallas_call(
        _moe_experts_kernel,
        out_shape=jax.ShapeDtypeStruct((rows, d), F32),
        grid_spec=grid_spec,
        compiler_params=_cparams(("arbitrary",)),
        name="moe_experts",
    )(tile_expert, n_valid, hs, wg, wu, wd)


def _moe_combine_kernel(d1_ref, d2_ref, x_ref, g_ref, route_ref, fw_ref, ys_ref, o_ref, buf1, buf2, sem,
                        *, final):
    tm = x_ref.shape[0]

    def start(r, _):
        _row_copy(ys_ref, d1_ref[0, r], buf1, r, sem).start()
        _row_copy(ys_ref, d2_ref[0, r], buf2, r, sem).start()
        return 0

    def wait(r, _):
        _row_copy(ys_ref, d1_ref[0, r], buf1, r, sem).wait()
        _row_copy(ys_ref, d2_ref[0, r], buf2, r, sem).wait()
        return 0

    lax.fori_loop(0, tm, start, 0, unroll=8)
    lax.fori_loop(0, tm, wait, 0, unroll=8)
    route = route_ref[...]
    f = route[:, _R_G1:_R_G1 + 1] * buf1[...] + route[:, _R_G2:_R_G2 + 1] * buf2[...]
    y = x_ref[...] + g_ref[...] * f
    o_ref[...] = _final_norm(y, fw_ref[...]) if final else y


def _moe_combine(x, mod, l, route, dest1, dest2, ys, fw, tm, per_row_mod, final):
    b, s, d = x.shape
    nt = s // tm
    _, _, g = _mod_specs(l, d, tm, per_row_mod, 2)
    idx = pl.BlockSpec((None, 1, tm), lambda i, t: (i * nt + t, 0, 0), memory_space=pltpu.SMEM)
    tok = lambda w: pl.BlockSpec((None, tm, w), lambda i, t: (i, t, 0))
    return pl.pallas_call(
        functools.partial(_moe_combine_kernel, final=final),
        out_shape=jax.ShapeDtypeStruct((b, s, d), F32),
        grid=(b, nt),
        in_specs=[idx, idx, tok(d), g, tok(LANES), pl.BlockSpec((1, d), lambda i, t: (0, 0)),
                  pl.BlockSpec(memory_space=pl.ANY)],
        out_specs=tok(d),
        scratch_shapes=[pltpu.VMEM((tm, d), F32), pltpu.VMEM((tm, d), F32), pltpu.SemaphoreType.DMA],
        compiler_params=_cparams(("arbitrary", "arbitrary")),
        name="moe_combine",
    )(dest1, dest2, x, mod, route, fw, ys)


def _moe(x, mod, l, nw, wr, wg, wu, wd, fw, tm, tg, per_row_mod, final):
    b, s, d = x.shape
    n = b * s
    h, route, counts = _moe_route(x, mod, l, nw, wr, tm, per_row_mod)
    cnt = counts[0, :N_EXPERTS].astype(jnp.int32)
    padded = (cnt + tg - 1) // tg * tg
    ends = jnp.cumsum(padded)
    offs = ends - padded
    rows = (2 * n + N_EXPERTS * (tg - 1) + tg - 1) // tg * tg
    n_valid = ends[-1] // tg
    tile = jnp.minimum(jnp.arange(rows // tg, dtype=jnp.int32), n_valid - 1)
    tile_expert = jnp.minimum(jnp.sum(tile[:, None] * tg >= ends[None, :], axis=1), N_EXPERTS - 1).astype(jnp.int32)
    rt = route.reshape(n, LANES)
    dest = lambda ke, kr: (offs[rt[:, ke].astype(jnp.int32)] + rt[:, kr].astype(jnp.int32)).reshape(n // tm, 1, tm)
    dest1, dest2 = dest(_R_E1, _R_R1), dest(_R_E2, _R_R2)
    zrow = jnp.concatenate([jnp.where(padded > 0, ends - tg, -1), n_valid[None]]).astype(jnp.int32)
    hs = _moe_dispatch(h.reshape(n, d), dest1, dest2, zrow, rows, tm, tg)
    ys = _moe_experts(hs, tile_expert, n_valid.reshape(1), wg, wu, wd, tg)
    return _moe_combine(x, mod, l, route, dest1, dest2, ys, fw, tm, per_row_mod, final)


def _page_cumsum_kernel(lf_ref, o_ref):
    n = lf_ref.shape[-1]
    r = lax.broadcasted_iota(jnp.int32, (n, n), 0)
    c = lax.broadcasted_iota(jnp.int32, (n, n), 1)
    o_ref[...] = _dot_tri(lf_ref[...], (r <= c).astype(BF16))


def _page_cumsum(lf_t):
    depth, h, n_pool, page = lf_t.shape
    return pl.pallas_call(
        _page_cumsum_kernel,
        out_shape=jax.ShapeDtypeStruct(lf_t.shape, F32),
        grid=(depth, h),
        in_specs=[pl.BlockSpec((None, None, n_pool, page), lambda l, i: (l, i, 0, 0))],
        out_specs=pl.BlockSpec((None, None, n_pool, page), lambda l, i: (l, i, 0, 0)),
        compiler_params=_cparams(("arbitrary", "arbitrary")),
        name="page_cumsum",
    )(lf_t)


def _fox_decode_kernel(pt_ref, q_ref, kn_ref, vn_ref, lfn_ref, k_hbm, v_hbm, f_hbm, o_ref,
                       pbuf, fbuf, psem, fsem, s_sc, acc, qb, *, l, nbuf):
    b = pl.program_id(0)
    n_pages = pt_ref.shape[1]
    page = pbuf.shape[-1]
    per_seq = 2 * n_pages
    total = pl.num_programs(0) * per_seq

    def page_copy(src, seq, j, slot):
        return pltpu.make_async_copy(src.at[l, pt_ref[seq, j]], pbuf.at[slot], psem.at[slot])

    def f_copy(seq, j, slot):
        pg = pt_ref[seq, j]
        src = f_hbm.at[l, :, lax.shift_right_logical(pg, 3), pl.ds(pg & 7, 1), :]
        return pltpu.make_async_copy(src, fbuf.at[slot], fsem.at[slot])

    def start_item(g, slot):
        seq = lax.div(g, per_seq)
        i = g - seq * per_seq

        @pl.when(i < n_pages)
        def _():
            page_copy(k_hbm, seq, i, slot).start()
            f_copy(seq, i, slot).start()

        @pl.when(i >= n_pages)
        def _():
            page_copy(v_hbm, seq, i - n_pages, slot).start()

    @pl.when(b == 0)
    def _():
        for s in range(nbuf):
            page_copy(k_hbm, 0, s, s).start()
            f_copy(0, s, s).start()

    q3 = q_ref[...].reshape(HEADS, HEAD_DIM, 1) * (HEAD_DIM ** -0.5)
    qb[...] = jnp.broadcast_to(q3, qb.shape)

    def k_body(i, run):
        slot = i & (nbuf - 1)
        page_copy(k_hbm, b, i, slot).wait()
        f_copy(b, i, slot).wait()
        floc = fbuf[slot]
        sc = jnp.sum(pbuf[slot] * qb[...], axis=1, keepdims=True)
        s_sc[:, :, pl.ds(pl.multiple_of(i * page, page), page)] = sc - (run + floc)
        nxt = b * per_seq + i + nbuf

        @pl.when(nxt < total)
        def _():
            start_item(nxt, slot)

        return run + floc[:, :, page - 1:page]

    run = lax.fori_loop(0, n_pages, k_body, jnp.zeros((HEADS, 1, 1), F32))

    s_all = s_sc[...]
    s_new = jnp.sum(q3 * kn_ref[...].reshape(HEADS, HEAD_DIM, 1), axis=1, keepdims=True) - (run + lfn_ref[...])
    m = jnp.maximum(jnp.max(s_all, axis=2, keepdims=True), s_new)
    p = jnp.exp(s_all - m)
    p_new = jnp.exp(s_new - m)
    s_sc[...] = p
    denom = jnp.sum(p, axis=2, keepdims=True) + p_new
    acc[...] = jnp.zeros_like(acc)

    def v_body(i, _):
        slot = i & (nbuf - 1)
        page_copy(v_hbm, b, i, slot).wait()
        acc[...] += pbuf[slot] * s_sc[:, :, pl.ds(pl.multiple_of(i * page, page), page)]
        nxt = b * per_seq + n_pages + i + nbuf

        @pl.when(nxt < total)
        def _():
            start_item(nxt, slot)

        return 0

    lax.fori_loop(0, n_pages, v_body, 0)
    o3 = jnp.sum(acc[...], axis=2, keepdims=True) + p_new * vn_ref[...].reshape(HEADS, HEAD_DIM, 1)
    o_ref[...] = (o3 / denom).reshape(MIX_W, 1)


def _fox_decode(page_table, q_col, kn_col, vn_col, lfn, k_t, v_t, floc, l, nbuf):
    nb, n_pages = page_table.shape
    page = k_t.shape[-1]
    assert nbuf <= n_pages and nbuf & (nbuf - 1) == 0 and n_pages % nbuf == 0
    col = pl.BlockSpec((None, MIX_W, 1), lambda i, pt: (i, 0, 0))
    hbm = pl.BlockSpec(memory_space=pl.ANY)
    grid_spec = pltpu.PrefetchScalarGridSpec(
        num_scalar_prefetch=1,
        grid=(nb,),
        in_specs=[col, col, col, pl.BlockSpec((None, HEADS, 1, 1), lambda i, pt: (i, 0, 0, 0)), hbm, hbm, hbm],
        out_specs=col,
        scratch_shapes=[pltpu.VMEM((nbuf, HEADS, HEAD_DIM, page), F32),
                        pltpu.VMEM((nbuf, HEADS, 1, page), F32),
                        pltpu.SemaphoreType.DMA((nbuf,)),
                        pltpu.SemaphoreType.DMA((nbuf,)),
                        pltpu.VMEM((HEADS, 1, n_pages * page), F32),
                        pltpu.VMEM((HEADS, HEAD_DIM, page), F32),
                        pltpu.VMEM((HEADS, HEAD_DIM, page), F32)],
    )
    return pl.pallas_call(
        functools.partial(_fox_decode_kernel, l=l, nbuf=nbuf),
        out_shape=jax.ShapeDtypeStruct((nb, MIX_W, 1), F32),
        grid_spec=grid_spec,
        compiler_params=_cparams(("arbitrary",)),
        name="fox_decode",
    )(page_table, q_col, kn_col, vn_col, lfn, k_t, v_t, floc)


def _mlstm_decode_kernel(q_ref, k_ref, v_ref, og_ref, li_ref, lf_ref, c_ref, n_ref, m_ref, nw_ref,
                         o_ref, c_out, n_out, m_out):
    q = q_ref[...]
    k = k_ref[...] * (HEAD_DIM ** -0.5)
    v = v_ref[...]
    c = c_ref[...]
    n = n_ref[...]
    m = m_ref[...]
    li = li_ref[...]
    lf = lf_ref[...]
    inter = lf + m
    m_t = jnp.maximum(inter, li)
    a_int = jnp.exp(inter - m_t)
    w_in = jnp.exp(li - m_t)
    s = jnp.sum(q * k, axis=1, keepdims=True) * w_in
    num = a_int * jnp.sum(q * c, axis=1, keepdims=True) + s * v
    den = a_int * jnp.sum(q * n, axis=1, keepdims=True) + s
    h = num / jnp.maximum(jnp.abs(den), jnp.exp(-m_t))
    h = h * lax.rsqrt(jnp.mean(h * h, axis=-1, keepdims=True) + EPS)
    o_ref[...] = _sigmoid(og_ref[...]) * (h * nw_ref[...])
    c_out[...] = a_int * c + w_in * (k * v)
    n_out[...] = a_int * n + w_in * k
    m_out[...] = m_t


def _mlstm_decode(q_col, k_col, v_row, og_row, li, lf, c0, n0_col, m0, nw_row):
    nb = q_col.shape[0]
    d = HEAD_DIM
    colspec = pl.BlockSpec((None, HEADS, d, 1), lambda i: (i, 0, 0, 0))
    rowspec = pl.BlockSpec((None, HEADS, 1, d), lambda i: (i, 0, 0, 0))
    sclspec = pl.BlockSpec((None, HEADS, 1, 1), lambda i: (i, 0, 0, 0))
    matspec = pl.BlockSpec((None, HEADS, d, d), lambda i: (i, 0, 0, 0))
    return pl.pallas_call(
        _mlstm_decode_kernel,
        out_shape=(jax.ShapeDtypeStruct((nb, HEADS, 1, d), F32),
                   jax.ShapeDtypeStruct((nb, HEADS, d, d), F32),
                   jax.ShapeDtypeStruct((nb, HEADS, d, 1), F32),
                   jax.ShapeDtypeStruct((nb, HEADS, 1, 1), F32)),
        grid=(nb,),
        in_specs=[colspec, colspec, rowspec, rowspec, sclspec, sclspec, matspec, colspec, sclspec,
                  pl.BlockSpec((HEADS, 1, d), lambda i: (0, 0, 0))],
        out_specs=(rowspec, matspec, colspec, sclspec),
        compiler_params=_cparams(("arbitrary",)),
        name="mlstm_decode",
    )(q_col, k_col, v_row, og_row, li, lf, c0, n0_col, m0, nw_row)


def _pool_decode_kernel(buf_ref, u_ref, wbd_ref, ps_ref, o_ref, nb_ref, *, pos):
    u = u_ref[...]
    n = buf_ref.shape[0]
    lane = lax.broadcasted_iota(jnp.int32, u.shape, 1)
    sums = {}
    run = u
    for back in range(1, max(POOL_WINDOWS)):
        run = run + buf_ref[n - back]
        if back + 1 in POOL_WINDOWS:
            sums[back + 1] = run
    cnt = {w: float(min(pos + 1, w)) for w in POOL_WINDOWS}
    mean = jnp.where(lane < 64, sums[2] / cnt[2], jnp.where(lane < 128, sums[4] / cnt[4],
                     jnp.where(lane < 192, sums[8] / cnt[8], sums[16] / cnt[16])))
    o_ref[...] = _mm(mean - u, wbd_ref[...]) * ps_ref[...]
    for r in range(n - 1):
        nb_ref[r] = buf_ref[r + 1]
    nb_ref[n - 1] = u


def _pool_decode(buf_t, u, wbd, pscale, l, pos):
    _, n, nb, w = buf_t.shape
    return pl.pallas_call(
        functools.partial(_pool_decode_kernel, pos=pos),
        out_shape=(jax.ShapeDtypeStruct((nb, w), F32), jax.ShapeDtypeStruct((n, nb, w), F32)),
        grid=(1,),
        in_specs=[pl.BlockSpec((None, n, nb, w), lambda i: (l, 0, 0, 0)),
                  pl.BlockSpec((nb, w), lambda i: (0, 0)),
                  pl.BlockSpec((None, w, w), lambda i: (l, 0, 0)),
                  pl.BlockSpec((None, 1, w), lambda i: (l, 0, 0))],
        out_specs=(pl.BlockSpec((nb, w), lambda i: (0, 0)),
                   pl.BlockSpec((n, nb, w), lambda i: (0, 0, 0))),
        compiler_params=_cparams(("arbitrary",)),
        name="pool_decode",
    )(buf_t, u, wbd, pscale)


def _tile(n, pref):
    t = min(n, pref)
    assert n % t == 0, (n, t)
    return t


def kernel(x_prompt, x_sample, c_prompt, c_sample, cache_fox_k, cache_fox_v, cache_fox_lf, page_table,
           state_mlstm_C, state_mlstm_n, state_mlstm_m, state_pool, w_ada, b_ada, norm1_w, norm2_w,
           w_in, b_fox_f, b_mlstm_i, b_mlstm_f, mlstm_norm_w, w_pool, pool_scale, w_out,
           w_ffn_gate, w_ffn_up, w_ffn_down, w_router, w_exp_gate, w_exp_up, w_exp_down, final_norm_w):
    depth = w_in.shape[0]
    b, s, d = x_prompt.shape
    nb = x_sample.shape[0]
    n_pool, page = cache_fox_k.shape[1], cache_fox_k.shape[2]
    assert x_sample.shape[1] == 1 and s % MLSTM_CHUNK == 0 and n_pool % 8 == 0

    pad_gate = lambda w: jnp.pad(w, ((0, 0), (0, 0), (0, LANES - HEADS)))
    w_in_f = jnp.concatenate(
        [w_in[:, :, :_MAIN_W], pad_gate(w_in[:, :, _MAIN_W:_MAIN_W + HEADS]),
         pad_gate(w_in[:, :, _MAIN_W + HEADS:_MAIN_W + 2 * HEADS]),
         pad_gate(w_in[:, :, _MAIN_W + 2 * HEADS:])], axis=-1)
    w_in_p = w_in_f.astype(BF16)
    gate_b = jnp.pad(jnp.stack([b_fox_f, b_mlstm_i, b_mlstm_f], axis=1), ((0, 0), (0, 0), (0, LANES - HEADS)))
    wbd_f = jnp.zeros((depth, POOL_W, POOL_W), F32)
    for g in range(len(POOL_WINDOWS)):
        sl = slice(g * HEAD_DIM, (g + 1) * HEAD_DIM)
        wbd_f = wbd_f.at[:, sl, sl].set(w_pool[:, g])
    wbd = wbd_f.astype(BF16)
    pscale = pool_scale.reshape(depth, 1, POOL_W)
    w_out_b = w_out.astype(BF16)
    nw1 = norm1_w.reshape(depth, 1, d)
    nw2 = norm2_w.reshape(depth, 1, d)
    mnw = mlstm_norm_w.reshape(depth, 1, MIX_W)
    mnw_row = mlstm_norm_w.reshape(depth, HEADS, 1, HEAD_DIM)
    fw = final_norm_w.reshape(1, d)
    w_router_p = jnp.pad(w_router, ((0, 0), (0, 0), (0, LANES - N_EXPERTS)))
    wfg, wfu, wfd = (w.astype(BF16) for w in (w_ffn_gate, w_ffn_up, w_ffn_down))
    weg, weu, wed = (w.astype(BF16) for w in (w_exp_gate, w_exp_up, w_exp_down))

    k_t = jnp.transpose(cache_fox_k, (0, 1, 3, 4, 2))
    v_t = jnp.transpose(cache_fox_v, (0, 1, 3, 4, 2))
    lf_t = jnp.transpose(cache_fox_lf, (0, 3, 1, 2))
    floc = _page_cumsum(lf_t).reshape(depth, HEADS, n_pool // 8, 8, page)
    pool_t = jnp.transpose(state_pool, (0, 2, 1, 3))

    mod = _ada_modulation(jnp.concatenate([c_prompt, c_sample], axis=0), w_ada, b_ada)
    mod_p = mod[:, :, :b].reshape(depth, 6, b, 1, d)
    mod_s = mod[:, :, b:]

    tm = _tile(s, 512)
    tq = _tile(s, 1024)
    tf = w_ffn_gate.shape[-1]
    xp = x_prompt
    xs = x_sample.reshape(1, nb, d)
    st_p, st_s = [], []
    for l in range(depth):
        last = l == depth - 1
        (qa, ka, va, kf, vf, lff, qm, km, vm, om, li, lfm, opool, ulast) = _in_proj_prompt(
            xp, mod_p, l, nw1, w_in_p, gate_b, wbd, pscale, tm)
        o_fox = _fox_prompt(qa, ka, va, tq)
        o_m, c_pair, n_pair, m_fin = _mlstm_prompt(qm, km, vm, om, li, lfm, mnw, l, 4 if b % 4 == 0 else 1)
        x1 = _out_proj(o_fox, o_m, opool, xp, mod_p, l, w_out_b, tm, False)
        if l % 2 == 0:
            j = l // 2
            xp = _ffn(x1, mod_p, l, nw2, wfg[j], wfu[j], wfd[j], fw, tm, tf, False, last)
        else:
            j = l // 2
            xp = _moe(x1, mod_p, l, nw2, w_router_p[j], weg[j], weu[j], wed[j], fw, tm, 256, False, last)
        c_fin = jnp.stack([c_pair[:, p, e * HEAD_DIM:(e + 1) * HEAD_DIM, e * HEAD_DIM:(e + 1) * HEAD_DIM]
                           for p in range(HEADS // 2) for e in range(2)], axis=1)
        st_p.append((kf.reshape(b, s, HEADS, HEAD_DIM), vf.reshape(b, s, HEADS, HEAD_DIM), lff[:, :, :HEADS],
                     c_fin, n_pair.reshape(b, HEADS, HEAD_DIM), m_fin[:, 0, :HEADS],
                     ulast[:, POOL_HALO - POOL_BUF:]))

        z, lff_s, li_s, lfm_s = _in_proj_sample(xs[0], mod_s, l, nw1, w_in_f, gate_b)
        seg = lambda i: z[:, i * MIX_W:(i + 1) * MIX_W]
        qf_s, kf_s, vf_s, qm_s, km_s, vm_s, om_s = (seg(i) for i in range(7))
        u_s = z[:, 7 * MIX_W:]
        col = lambda a: a.reshape(nb, MIX_W, 1)
        o_fox_s = _fox_decode(page_table, col(qf_s), col(kf_s), col(vf_s),
                              lff_s[:, :HEADS].reshape(nb, HEADS, 1, 1), k_t, v_t, floc, l,
                              min(32, page_table.shape[1]))
        hcol = lambda a: a.reshape(nb, HEADS, HEAD_DIM, 1)
        hrow = lambda a: a.reshape(nb, HEADS, 1, HEAD_DIM)
        hscl = lambda a: a[:, :HEADS].reshape(nb, HEADS, 1, 1)
        o_m_s, c_new, n_new, m_new = _mlstm_decode(
            hcol(qm_s), hcol(km_s), hrow(vm_s), hrow(om_s), hscl(li_s), hscl(lfm_s),
            state_mlstm_C[l], hcol(state_mlstm_n[l]), state_mlstm_m[l].reshape(nb, HEADS, 1, 1), mnw_row[l])
        o_pool_s, buf_new = _pool_decode(pool_t, u_s, wbd_f, pscale, l, page_table.shape[1] * page)
        mix = lambda a: a.reshape(1, nb, -1)
        x1s = _out_proj(mix(o_fox_s), mix(o_m_s), mix(o_pool_s), xs, mod_s, l, w_out, nb, True)
        if l % 2 == 0:
            xs = _ffn(x1s, mod_s, l, nw2, w_ffn_gate[j], w_ffn_up[j], w_ffn_down[j], fw, nb, 2 * LANES, True, last)
        else:
            xs = _moe(x1s, mod_s, l, nw2, w_router_p[j], weg[j], weu[j], wed[j], fw, nb, 8, True, last)
        st_s.append((kf_s.reshape(nb, 1, HEADS, HEAD_DIM), vf_s.reshape(nb, 1, HEADS, HEAD_DIM),
                     lff_s[:, :HEADS].reshape(nb, 1, HEADS), c_new, n_new.reshape(nb, HEADS, HEAD_DIM),
                     m_new.reshape(nb, HEADS), jnp.transpose(buf_new, (1, 0, 2))))

    outs_p = [jnp.stack(a) for a in zip(*st_p)]
    outs_s = [jnp.stack(a) for a in zip(*st_s)]
    return (xp, xs.reshape(nb, 1, d), *outs_p, *outs_s)
```

```python
import functools

import jax
import jax.numpy as jnp
from jax import lax
from jax.experimental import pallas as pl
from jax.experimental.pallas import tpu as pltpu

F32 = jnp.float32
BF16 = jnp.bfloat16

HEAD_DIM = 64
LANES = 128
MXU_TILE = 256
HEADS = 6
MIX_W = HEADS * HEAD_DIM
POOL_WINDOWS = (2, 4, 8, 16)
POOL_W = 256
POOL_HALO = 16
POOL_BUF = 15
MLSTM_CHUNK = 128
N_EXPERTS = 8
EPS = 1e-6
NEG_INF = float("-inf")

_MAIN_W = 7 * MIX_W + POOL_W
_GATE_OFF = _MAIN_W
IN_PAD_W = _MAIN_W + 3 * LANES

VMEM_LIMIT = 56 * 1024 * 1024


def _cparams(sem):
    return pltpu.CompilerParams(dimension_semantics=sem, vmem_limit_bytes=VMEM_LIMIT)


def _sigmoid(x):
    return 1.0 / (1.0 + jnp.exp(-x))


def _log_sigmoid(x):
    return jnp.minimum(x, 0.0) - jnp.log(1.0 + jnp.exp(-jnp.abs(x)))


def _split3(x):
    hi = x.astype(BF16).astype(F32)
    r = x - hi
    mid = r.astype(BF16).astype(F32)
    lo = (r - mid).astype(BF16).astype(F32)
    return hi, mid, lo


def _dot(a, b):
    return jnp.dot(a, b, preferred_element_type=F32)


def _dot_nt(a, b):
    return lax.dot_general(a, b, (((1,), (1,)), ((), ())), preferred_element_type=F32)


def _dot_tn(a, b):
    return lax.dot_general(a, b, (((0,), (0,)), ((), ())), preferred_element_type=F32)


def _tri_dot(tri_bf16, x):
    hi, mid, lo = _split3(x)
    return (_dot(tri_bf16, hi.astype(BF16)) + _dot(tri_bf16, mid.astype(BF16))
            + _dot(tri_bf16, lo.astype(BF16)))


def _dot_tri(x, tri_bf16):
    hi, mid, lo = _split3(x)
    return (_dot(hi.astype(BF16), tri_bf16) + _dot(mid.astype(BF16), tri_bf16)
            + _dot(lo.astype(BF16), tri_bf16))


def _dot_f32(a, b, passes=6):
    a0, a1, a2 = (p.astype(BF16) for p in _split3(a))
    b0, b1, b2 = (p.astype(BF16) for p in _split3(b))
    out = _dot(a0, b1) + _dot(a1, b0)
    if passes == 6:
        out = (_dot(a0, b2) + _dot(a2, b0) + _dot(a1, b1)) + out
    return out + _dot(a0, b0)


def _mm(a, w):
    if w.dtype == F32:
        return _dot_f32(a.astype(F32), w)
    return _dot(a.astype(BF16), w)


def _lower_tri(n):
    r = lax.broadcasted_iota(jnp.int32, (n, n), 0)
    c = lax.broadcasted_iota(jnp.int32, (n, n), 1)
    return r >= c


def _modulated_rmsnorm(x, nw, sc, sh):
    y = x * lax.rsqrt(jnp.mean(x * x, axis=-1, keepdims=True) + EPS)
    return (y * nw) * (1.0 + sc) + sh


def _ada_kernel(c_ref, w_ref, b_ref, o_ref):
    c = c_ref[...]
    s = c * _sigmoid(c)
    o_ref[...] = _mm(s, w_ref[...]) + b_ref[...]


def _ada_modulation(c_all, w_ada, b_ada):
    depth, d, _ = w_ada.shape
    r = c_all.shape[0]
    return pl.pallas_call(
        _ada_kernel,
        out_shape=jax.ShapeDtypeStruct((depth, 6, r, d), F32),
        grid=(depth, 6),
        in_specs=[
            pl.BlockSpec((r, d), lambda l, j: (0, 0)),
            pl.BlockSpec((None, d, d), lambda l, j: (l, 0, j)),
            pl.BlockSpec((None, 1, d), lambda l, j: (l, 0, j)),
        ],
        out_specs=pl.BlockSpec((None, None, r, d), lambda l, j: (l, j, 0, 0)),
        compiler_params=_cparams(("arbitrary", "arbitrary")),
        name="ada_modulation",
    )(c_all, w_ada, b_ada.reshape(depth, 1, 6 * d))


def _pool_mix(u, halo, pos0, wbd, scale):
    t = u.shape[0]
    ext = jnp.concatenate([halo, u], axis=0)
    p2 = ext + pltpu.roll(ext, 1, 0)
    p4 = p2 + pltpu.roll(p2, 2, 0)
    p8 = p4 + pltpu.roll(p4, 4, 0)
    p16 = p8 + pltpu.roll(p8, 8, 0)
    lane = lax.broadcasted_iota(jnp.int32, (t, POOL_W), 1)
    sums = jnp.where(lane < 64, p2[POOL_HALO:], jnp.where(lane < 128, p4[POOL_HALO:],
                     jnp.where(lane < 192, p8[POOL_HALO:], p16[POOL_HALO:])))
    win = jnp.where(lane < 64, 2.0, jnp.where(lane < 128, 4.0, jnp.where(lane < 192, 8.0, 16.0)))
    pos = (pos0 + lax.broadcasted_iota(jnp.int32, (t, POOL_W), 0)).astype(F32)
    cnt = jnp.minimum(pos + 1.0, win)
    d = sums / cnt - u
    return _dot(d.astype(BF16), wbd) * scale


def _in_proj_prompt_kernel(x_ref, nw_ref, sc_ref, sh_ref, w_ref, gb_ref, wbd_ref, ps_ref,
                           qa_ref, ka_ref, va_ref, kf_ref, vf_ref, lff_ref,
                           qm_ref, km_ref, vm_ref, om_ref, li_ref, lfm_ref, op_ref, ul_ref,
                           fcarry, uhalo):
    t = pl.program_id(1)
    tm = x_ref.shape[0]

    @pl.when(t == 0)
    def _():
        fcarry[...] = jnp.zeros_like(fcarry)
        uhalo[...] = jnp.zeros_like(uhalo)

    hb = _modulated_rmsnorm(x_ref[...], nw_ref[...], sc_ref[...], sh_ref[...]).astype(BF16)
    split = 4 * MIX_W
    za = _dot(hb, w_ref[:, 0:split])
    zb = _dot(hb, w_ref[:, split:IN_PAD_W])
    piece_a = lambda i: za[:, i * MIX_W:(i + 1) * MIX_W]
    piece_b = lambda i: zb[:, i * MIX_W:(i + 1) * MIX_W]

    lane = lax.broadcasted_iota(jnp.int32, (tm, LANES), 1)
    zg = zb[:, _GATE_OFF - split:IN_PAD_W - split]
    head_lane = lane < HEADS
    lf_fox = jnp.where(head_lane, _log_sigmoid(zg[:, 0:LANES] + gb_ref[0:1, :]), 0.0)
    li = jnp.where(head_lane, zg[:, LANES:2 * LANES] + gb_ref[1:2, :], 0.0)
    lf_m = jnp.where(head_lane, _log_sigmoid(zg[:, 2 * LANES:3 * LANES] + gb_ref[2:3, :]), 0.0)
    lff_ref[...] = lf_fox
    li_ref[...] = li
    lfm_ref[...] = lf_m

    cum = _tri_dot(_lower_tri(tm).astype(BF16), lf_fox) + fcarry[...]
    fcarry[...] = cum[tm - 1:tm, :]

    zq = piece_a(0) * (HEAD_DIM ** -0.5)
    zk = piece_a(1)
    zv = piece_a(2)
    kf_ref[...] = zk
    vf_ref[...] = zv
    low = lane < HEAD_DIM
    for h in range(HEADS):
        p, e = divmod(h, 2)
        sl = slice(p * LANES, (p + 1) * LANES)
        bq, bk, bv = zq[:, sl], zk[:, sl], zv[:, sl]
        if e == 1:
            bq, bk, bv = (pltpu.roll(a, HEAD_DIM, 1) for a in (bq, bk, bv))
        hi, mid, lo = _split3(cum[:, h:h + 1])
        augq = jnp.where(lane == 64, hi, jnp.where(lane == 65, mid, jnp.where(lane == 66, lo,
                         jnp.where(lane < 70, 1.0, 0.0))))
        augk = jnp.where(lane < 67, 1.0, jnp.where(lane == 67, -hi, jnp.where(lane == 68, -mid,
                         jnp.where(lane == 69, -lo, 0.0))))
        augv = jnp.where(lane == 64, 1.0, 0.0)
        qa_ref[h] = jnp.where(low, bq, augq).astype(BF16)
        ka_ref[h] = jnp.where(low, bk, augk).astype(BF16)
        va_ref[h] = jnp.where(low, bv, augv).astype(BF16)

    qm_ref[...] = piece_a(3).astype(BF16)
    km_ref[...] = (piece_b(0) * (HEAD_DIM ** -0.5)).astype(BF16)
    vm_ref[...] = piece_b(1).astype(BF16)
    om_ref[...] = piece_b(2)

    u = zb[:, 3 * MIX_W:3 * MIX_W + POOL_W]
    op_ref[...] = _pool_mix(u, uhalo[...], t * tm, wbd_ref[...], ps_ref[...]).astype(BF16)
    uhalo[...] = u[tm - POOL_HALO:, :]
    ul_ref[...] = u[tm - POOL_HALO:, :]


def _in_proj_prompt(x, mod, l, nw, w_in_p, gate_b, wbd, pscale, tm):
    b, s, d = x.shape
    nt = s // tm
    row = lambda width, dt: jax.ShapeDtypeStruct((b, s, width), dt)
    head = jax.ShapeDtypeStruct((b, HEADS, s, LANES), BF16)
    tok = lambda width: pl.BlockSpec((None, tm, width), lambda i, t: (i, t, 0))
    hspec = pl.BlockSpec((None, HEADS, tm, LANES), lambda i, t: (i, 0, t, 0))
    modspec = lambda j: pl.BlockSpec((None, None, None, 1, d), lambda i, t: (l, j, i, 0, 0))
    return pl.pallas_call(
        _in_proj_prompt_kernel,
        out_shape=(head, head, head, row(MIX_W, F32), row(MIX_W, F32), row(LANES, F32),
                   row(MIX_W, BF16), row(MIX_W, BF16), row(MIX_W, BF16), row(MIX_W, F32),
                   row(LANES, F32), row(LANES, F32), row(POOL_W, BF16),
                   jax.ShapeDtypeStruct((b, POOL_HALO, POOL_W), F32)),
        grid=(b, nt),
        in_specs=[
            tok(d),
            pl.BlockSpec((None, 1, d), lambda i, t: (l, 0, 0)),
            modspec(1), modspec(0),
            pl.BlockSpec((None, d, IN_PAD_W), lambda i, t: (l, 0, 0)),
            pl.BlockSpec((None, 3, LANES), lambda i, t: (l, 0, 0)),
            pl.BlockSpec((None, POOL_W, POOL_W), lambda i, t: (l, 0, 0)),
            pl.BlockSpec((None, 1, POOL_W), lambda i, t: (l, 0, 0)),
        ],
        out_specs=(hspec, hspec, hspec, tok(MIX_W), tok(MIX_W), tok(LANES),
                   tok(MIX_W), tok(MIX_W), tok(MIX_W), tok(MIX_W), tok(LANES), tok(LANES),
                   tok(POOL_W), pl.BlockSpec((None, POOL_HALO, POOL_W), lambda i, t: (i, 0, 0))),
        scratch_shapes=[pltpu.VMEM((1, LANES), F32), pltpu.VMEM((POOL_HALO, POOL_W), F32)],
        compiler_params=_cparams(("arbitrary", "arbitrary")),
        name="in_proj_prompt",
    )(x, nw, mod, mod, w_in_p, gate_b, wbd, pscale)


def _in_proj_sample_kernel(x_ref, nw_ref, sc_ref, sh_ref, w_ref, gb_ref, z_ref, lff_ref, li_ref, lfm_ref):
    h = _modulated_rmsnorm(x_ref[...], nw_ref[...], sc_ref[...], sh_ref[...])
    chunk = 4 * LANES
    for c0 in range(0, _MAIN_W, chunk):
        c1 = min(c0 + chunk, _MAIN_W)
        z_ref[:, c0:c1] = _mm(h, w_ref[:, c0:c1])
    zg = _mm(h, w_ref[:, _GATE_OFF:IN_PAD_W])
    lff_ref[...] = _log_sigmoid(zg[:, 0:LANES] + gb_ref[0:1, :])
    li_ref[...] = zg[:, LANES:2 * LANES] + gb_ref[1:2, :]
    lfm_ref[...] = _log_sigmoid(zg[:, 2 * LANES:3 * LANES] + gb_ref[2:3, :])


def _in_proj_sample(x, mod, l, nw, w_in_p, gate_b):
    n, d = x.shape
    gate = jax.ShapeDtypeStruct((n, LANES), F32)
    full = lambda *shape: pl.BlockSpec(shape, lambda i: (0,) * len(shape))
    modspec = lambda j: pl.BlockSpec((None, None, n, d), lambda i: (l, j, 0, 0))
    return pl.pallas_call(
        _in_proj_sample_kernel,
        out_shape=(jax.ShapeDtypeStruct((n, _MAIN_W), F32), gate, gate, gate),
        grid=(1,),
        in_specs=[
            full(n, d),
            pl.BlockSpec((None, 1, d), lambda i: (l, 0, 0)),
            modspec(1), modspec(0),
            pl.BlockSpec((None, d, IN_PAD_W), lambda i: (l, 0, 0)),
            pl.BlockSpec((None, 3, LANES), lambda i: (l, 0, 0)),
        ],
        out_specs=(full(n, _MAIN_W), full(n, LANES), full(n, LANES), full(n, LANES)),
        compiler_params=_cparams(("arbitrary",)),
        name="in_proj_sample",
    )(x, nw, mod, mod, w_in_p, gate_b)


def _fox_prompt_kernel(qa_ref, ka_ref, va_ref, o_ref, *, tq):
    qi = pl.program_id(2)
    q = (qa_ref[0], qa_ref[1])

    def update(qe, e, start, size, carry, causal):
        m, acc = carry
        k = ka_ref[e, pl.ds(start, size), :]
        v = va_ref[e, pl.ds(start, size), :]
        s = _dot_nt(qe, k)
        if causal:
            r = lax.broadcasted_iota(jnp.int32, s.shape, 0)
            c = lax.broadcasted_iota(jnp.int32, s.shape, 1)
            s = jnp.where(c <= r, s, NEG_INF)
        m_new = jnp.maximum(m, jnp.max(s, axis=-1, keepdims=True))
        p = jnp.exp(s - m_new)
        return m_new, acc * jnp.exp(m - m_new) + _dot(p.astype(BF16), v)

    def body(ki, carry):
        start = pl.multiple_of(ki * tq, tq)
        return tuple(update(q[e], e, start, tq, carry[e], False) for e in range(2))

    init = (jnp.full((tq, 1), NEG_INF, F32), jnp.zeros((tq, LANES), F32))
    carry = lax.fori_loop(0, qi, body, (init, init))
    half = tq // 2
    base = pl.multiple_of(qi * tq, tq)
    outs = []
    for e in range(2):
        m, acc = update(q[e], e, base, half, carry[e], True)
        _, acc_lo = update(q[e][half:], e, base + half, half, (m[half:], acc[half:]), True)
        acc = jnp.concatenate([acc[:half], acc_lo], axis=0)
        outs.append(acc / acc[:, HEAD_DIM:HEAD_DIM + 1])
    lane = lax.broadcasted_iota(jnp.int32, (tq, LANES), 1)
    o_ref[...] = jnp.where(lane < HEAD_DIM, outs[0], pltpu.roll(outs[1], HEAD_DIM, 1)).astype(o_ref.dtype)


def _fox_prompt(qa, ka, va, tq):
    b, _, s, _ = qa.shape
    return pl.pallas_call(
        functools.partial(_fox_prompt_kernel, tq=tq),
        out_shape=jax.ShapeDtypeStruct((b, s, MIX_W), BF16),
        grid=(b, HEADS // 2, s // tq),
        in_specs=[
            pl.BlockSpec((None, 2, tq, LANES), lambda i, p, t: (i, p, t, 0)),
            pl.BlockSpec((None, 2, s, LANES), lambda i, p, t: (i, p, 0, 0)),
            pl.BlockSpec((None, 2, s, LANES), lambda i, p, t: (i, p, 0, 0)),
        ],
        out_specs=pl.BlockSpec((None, tq, LANES), lambda i, p, t: (i, t, p)),
        compiler_params=_cparams(("arbitrary", "arbitrary", "arbitrary")),
        name="fox_prompt",
    )(qa, ka, va)


def _mlstm_prompt_kernel(q_ref, k_ref, v_ref, og_ref, li_ref, lf_ref, nw_ref,
                         o_ref, c_out, n_out, m_out, c_st, n_st, m_st):
    c = pl.program_id(1)
    nb, L, _ = q_ref.shape

    @pl.when(c == 0)
    def _():
        c_st[...] = jnp.zeros_like(c_st)
        n_st[...] = jnp.zeros_like(n_st)
        m_st[...] = jnp.zeros_like(m_st)

    tril = _lower_tri(L)
    tri = tril.astype(BF16)
    lane = lax.broadcasted_iota(jnp.int32, (L, LANES), 1)
    lane1 = lax.broadcasted_iota(jnp.int32, (1, LANES), 1)
    rowi = lax.broadcasted_iota(jnp.int32, (LANES, LANES), 0)
    coli = lax.broadcasted_iota(jnp.int32, (LANES, LANES), 1)
    same_head = (rowi < HEAD_DIM) == (coli < HEAD_DIM)

    for bi in range(nb):
        lf = lf_ref[bi]
        li = li_ref[bi]
        bcum = _tri_dot(tri, lf)
        m_prev = m_st[bi]
        b_end = bcum[L - 1:L, :]
        g = b_end - bcum + li
        m_new = jnp.maximum(b_end + m_prev, jnp.max(g, axis=0, keepdims=True))
        a_state = jnp.exp(b_end + m_prev - m_new)
        wg = jnp.exp(g - m_new)
        inter = bcum + m_prev
        bcum_t = bcum.T
        li_t = li.T
        heads = range(HEADS)
        pairs = [slice(p * LANES, (p + 1) * LANES) for p in range(HEADS // 2)]
        qs = [q_ref[bi, :, sl] for sl in pairs]
        ks = [k_ref[bi, :, sl] for sl in pairs]
        vs = [v_ref[bi, :, sl] for sl in pairs]
        mine = [(lane < HEAD_DIM) if h % 2 == 0 else (lane >= HEAD_DIM) for h in heads]
        own = jnp.stack(mine)
        q6 = jnp.stack([jnp.where(mine[h], qs[h // 2], jnp.zeros_like(qs[0])) for h in heads])
        k6 = jnp.stack([ks[h // 2] for h in heads])
        v6 = jnp.stack([vs[h // 2] for h in heads])
        n6 = jnp.stack([n_st[bi, h // 2] for h in heads])
        qc3 = [_dot(qs[p], c_st[bi, p].astype(BF16)) for p in range(HEADS // 2)]
        qc6 = jnp.stack([qc3[h // 2] for h in heads])
        bcol = jnp.stack([bcum[:, h:h + 1] for h in heads])
        icol = jnp.stack([inter[:, h:h + 1] for h in heads])
        brow = jnp.stack([bcum_t[h:h + 1, :] for h in heads])
        lrow = jnp.stack([li_t[h:h + 1, :] for h in heads])
        dmat = jnp.where(tril[None], bcol - brow + lrow, NEG_INF)
        m_t = jnp.maximum(icol, jnp.max(dmat, axis=-1, keepdims=True))
        a_int = jnp.exp(icol - m_t)
        s = lax.dot_general(q6, k6, (((2,), (2,)), ((0,), (0,))), preferred_element_type=F32) \
            * jnp.exp(dmat - m_t)
        sv = lax.dot_general(s.astype(BF16), v6, (((2,), (1,)), ((0,), (0,))), preferred_element_type=F32)
        den = a_int * jnp.sum(q6.astype(F32) * n6, axis=-1, keepdims=True) + jnp.sum(s, axis=-1, keepdims=True)
        hh = (a_int * qc6 + sv) / jnp.maximum(jnp.abs(den), jnp.exp(-m_t))
        ms = jnp.sum(jnp.where(own, hh * hh, 0.0), axis=-1, keepdims=True) * (1.0 / HEAD_DIM)
        hn = hh * lax.rsqrt(ms + EPS)
        for p in range(HEADS // 2):
            sl = pairs[p]
            k, v = ks[p], vs[p]
            cp = c_st[bi, p]
            npair = n_st[bi, p]
            hpair = jnp.where(lane < HEAD_DIM, hn[2 * p], hn[2 * p + 1])
            o_ref[bi, :, sl] = (_sigmoid(og_ref[bi, :, sl]) * (hpair * nw_ref[:, sl])).astype(o_ref.dtype)
            wgp = jnp.where(lane < HEAD_DIM, wg[:, 2 * p:2 * p + 1], wg[:, 2 * p + 1:2 * p + 2])
            ap = jnp.where(lane1 < HEAD_DIM, a_state[:, 2 * p:2 * p + 1], a_state[:, 2 * p + 1:2 * p + 2])
            kv = _dot_tn(k, (v.astype(F32) * wgp).astype(BF16))
            c_st[bi, p] = cp * ap + jnp.where(same_head, kv, 0.0)
            n_st[bi, p] = npair * ap + jnp.sum(k.astype(F32) * wgp, axis=0, keepdims=True)
        m_st[bi] = m_new

    @pl.when(c == pl.num_programs(1) - 1)
    def _():
        c_out[...] = c_st[...]
        n_out[...] = n_st[...]
        m_out[...] = m_st[...]


def _mlstm_prompt(qm, km, vm, om, li, lfm, nw, l, nb):
    b, s, _ = qm.shape
    L = MLSTM_CHUNK
    np_ = HEADS // 2
    tok = lambda w: pl.BlockSpec((nb, L, w), lambda i, c: (i, c, 0))
    return pl.pallas_call(
        _mlstm_prompt_kernel,
        out_shape=(jax.ShapeDtypeStruct((b, s, MIX_W), BF16),
                   jax.ShapeDtypeStruct((b, np_, LANES, LANES), F32),
                   jax.ShapeDtypeStruct((b, np_, 1, LANES), F32),
                   jax.ShapeDtypeStruct((b, 1, LANES), F32)),
        grid=(b // nb, s // L),
        in_specs=[tok(MIX_W), tok(MIX_W), tok(MIX_W), tok(MIX_W), tok(LANES), tok(LANES),
                  pl.BlockSpec((None, 1, MIX_W), lambda i, c: (l, 0, 0))],
        out_specs=(tok(MIX_W),
                   pl.BlockSpec((nb, np_, LANES, LANES), lambda i, c: (i, 0, 0, 0)),
                   pl.BlockSpec((nb, np_, 1, LANES), lambda i, c: (i, 0, 0, 0)),
                   pl.BlockSpec((nb, 1, LANES), lambda i, c: (i, 0, 0))),
        scratch_shapes=[pltpu.VMEM((nb, np_, LANES, LANES), F32),
                        pltpu.VMEM((nb, np_, 1, LANES), F32),
                        pltpu.VMEM((nb, 1, LANES), F32)],
        compiler_params=_cparams(("arbitrary", "arbitrary")),
        name="mlstm_prompt",
    )(qm, km, vm, om, li, lfm, nw)


def _out_proj_kernel(of_ref, om_ref, op_ref, x_ref, g_ref, w_ref, o_ref):
    y = _mm(jnp.concatenate([of_ref[...], om_ref[...], op_ref[...]], axis=1), w_ref[...])
    o_ref[...] = x_ref[...] + g_ref[...] * y


def _out_proj(of, om, op, x, mod, l, w_out_b, tm, per_row_mod):
    b, s, d = x.shape
    tok = lambda w: pl.BlockSpec((None, tm, w), lambda i, t: (i, t, 0))
    if per_row_mod:
        gspec = pl.BlockSpec((None, None, tm, d), lambda i, t: (l, 2, 0, 0))
    else:
        gspec = pl.BlockSpec((None, None, None, 1, d), lambda i, t: (l, 2, i, 0, 0))
    return pl.pallas_call(
        _out_proj_kernel,
        out_shape=jax.ShapeDtypeStruct((b, s, d), F32),
        grid=(b, s // tm),
        in_specs=[tok(MIX_W), tok(MIX_W), tok(POOL_W), tok(d), gspec,
                  pl.BlockSpec((None, d, d), lambda i, t: (l, 0, 0))],
        out_specs=tok(d),
        compiler_params=_cparams(("arbitrary", "arbitrary")),
        name="out_proj",
    )(of, om, op, x, mod, w_out_b)


def _final_norm(x, fw):
    return (x * lax.rsqrt(jnp.mean(x * x, axis=-1, keepdims=True) + EPS)) * fw


def _swiglu_chunks(h, wg_ref, wu_ref, wd_ref, chunk=MXU_TILE):
    ff = wg_ref.shape[-1]
    y = None
    for c0 in range(0, ff, chunk):
        c1 = min(c0 + chunk, ff)
        a = _mm(h, wg_ref[:, c0:c1])
        u = _mm(h, wu_ref[:, c0:c1])
        part = _mm((a * _sigmoid(a)) * u, wd_ref[c0:c1, :])
        y = part if y is None else y + part
    return y


def _ffn_kernel(x_ref, nw_ref, sc_ref, sh_ref, g_ref, wg_ref, wu_ref, wd_ref, fw_ref, o_ref,
                h_sc, acc, *, final):
    j = pl.program_id(2)

    @pl.when(j == 0)
    def _():
        h_sc[...] = _modulated_rmsnorm(x_ref[...], nw_ref[...], sc_ref[...], sh_ref[...]).astype(h_sc.dtype)
        acc[...] = jnp.zeros_like(acc)

    h = h_sc[...]
    acc[...] += _swiglu_chunks(h, wg_ref, wu_ref, wd_ref)

    @pl.when(j == pl.num_programs(2) - 1)
    def _():
        y = x_ref[...] + g_ref[...] * acc[...]
        o_ref[...] = _final_norm(y, fw_ref[...]) if final else y


def _mod_specs(l, d, tm, per_row_mod, nidx):
    def spec(j):
        if per_row_mod:
            return pl.BlockSpec((None, None, tm, d), lambda *a: (l, j, 0, 0))
        return pl.BlockSpec((None, None, None, 1, d), lambda *a: (l, j, a[0], 0, 0))
    return spec(4), spec(3), spec(5)


def _ffn(x, mod, l, nw, wg, wu, wd, fw, tm, tf, per_row_mod, final):
    b, s, d = x.shape
    ff = wg.shape[-1]
    sc, sh, g = _mod_specs(l, d, tm, per_row_mod, 3)
    return pl.pallas_call(
        functools.partial(_ffn_kernel, final=final),
        out_shape=jax.ShapeDtypeStruct((b, s, d), F32),
        grid=(b, s // tm, ff // tf),
        in_specs=[
            pl.BlockSpec((None, tm, d), lambda i, t, j: (i, t, 0)),
            pl.BlockSpec((None, 1, d), lambda i, t, j: (l, 0, 0)),
            sc, sh, g,
            pl.BlockSpec((d, tf), lambda i, t, j: (0, j)),
            pl.BlockSpec((d, tf), lambda i, t, j: (0, j)),
            pl.BlockSpec((tf, d), lambda i, t, j: (j, 0)),
            pl.BlockSpec((1, d), lambda i, t, j: (0, 0)),
        ],
        out_specs=pl.BlockSpec((None, tm, d), lambda i, t, j: (i, t, 0)),
        scratch_shapes=[pltpu.VMEM((tm, d), wg.dtype), pltpu.VMEM((tm, d), F32)],
        compiler_params=_cparams(("arbitrary", "arbitrary", "arbitrary")),
        name="ffn_swiglu",
    )(x, nw, mod, mod, mod, wg, wu, wd, fw)


def _top2(logits):
    lane = lax.broadcasted_iota(jnp.int32, logits.shape, 1)
    valid = lane < N_EXPERTS
    z = jnp.where(valid, logits, NEG_INF)
    pz = jnp.exp(z - jnp.max(z, axis=-1, keepdims=True))
    probs = pz / jnp.sum(pz, axis=-1, keepdims=True)
    m1 = jnp.max(probs, axis=-1, keepdims=True)
    i1 = jnp.min(jnp.where(probs == m1, lane, LANES), axis=-1, keepdims=True)
    rest = jnp.where((lane == i1) | (~valid), -1.0, probs)
    m2 = jnp.max(rest, axis=-1, keepdims=True)
    i2 = jnp.min(jnp.where(rest == m2, lane, LANES), axis=-1, keepdims=True)
    tot = m1 + m2
    return i1, i2, m1 / tot, m2 / tot


_R_E1, _R_E2, _R_R1, _R_R2, _R_G1, _R_G2 = range(6)


def _moe_route_kernel(x_ref, nw_ref, sc_ref, sh_ref, wr_ref, h_ref, route_ref, cnt_ref, carry):
    first = (pl.program_id(0) == 0) & (pl.program_id(1) == 0)

    @pl.when(first)
    def _():
        carry[...] = jnp.zeros_like(carry)

    h = _modulated_rmsnorm(x_ref[...], nw_ref[...], sc_ref[...], sh_ref[...])
    h_ref[...] = h
    tm = h.shape[0]
    i1, i2, g1, g2 = _top2(_dot_f32(h, wr_ref[...], passes=3))
    lane = lax.broadcasted_iota(jnp.int32, (tm, LANES), 1)
    oh1 = lane == i1
    oh2 = lane == i2
    both = jnp.where(oh1 | oh2, 1.0, 0.0)
    r = lax.broadcasted_iota(jnp.int32, (tm, tm), 0)
    c = lax.broadcasted_iota(jnp.int32, (tm, tm), 1)
    before = carry[...] + _dot((r > c).astype(BF16), both.astype(BF16))
    r1 = jnp.sum(jnp.where(oh1, before, 0.0), axis=-1, keepdims=True)
    r2 = jnp.sum(jnp.where(oh2, before, 0.0), axis=-1, keepdims=True)
    carry[...] += jnp.sum(both, axis=0, keepdims=True)
    cnt_ref[...] = carry[...]
    vals = (i1.astype(F32), i2.astype(F32), r1, r2, g1, g2)
    route = jnp.zeros((tm, LANES), F32)
    for k, v in enumerate(vals):
        route = jnp.where(lane == k, v, route)
    route_ref[...] = route


def _moe_route(x, mod, l, nw, wr, tm, per_row_mod):
    b, s, d = x.shape
    sc, sh, _ = _mod_specs(l, d, tm, per_row_mod, 2)
    tok = lambda w: pl.BlockSpec((None, tm, w), lambda i, t: (i, t, 0))
    return pl.pallas_call(
        _moe_route_kernel,
        out_shape=(jax.ShapeDtypeStruct((b, s, d), F32), jax.ShapeDtypeStruct((b, s, LANES), F32),
                   jax.ShapeDtypeStruct((1, LANES), F32)),
        grid=(b, s // tm),
        in_specs=[tok(d), pl.BlockSpec((None, 1, d), lambda i, t: (l, 0, 0)), sc, sh,
                  pl.BlockSpec((d, LANES), lambda i, t: (0, 0))],
        out_specs=(tok(d), tok(LANES), pl.BlockSpec((1, LANES), lambda i, t: (0, 0))),
        scratch_shapes=[pltpu.VMEM((1, LANES), F32)],
        compiler_params=_cparams(("arbitrary", "arbitrary")),
        name="moe_route",
    )(x, nw, mod, mod, wr)


def _row_copy(src, src_row, dst, dst_row, sem):
    return pltpu.make_async_copy(src.at[pl.ds(src_row, 1)], dst.at[pl.ds(dst_row, 1)], sem)


def _moe_dispatch_kernel(zrow_ref, d1_ref, d2_ref, h_ref, hs_ref, zbuf, sem, zsem):
    tm = h_ref.shape[0]
    tg = zbuf.shape[0]

    @pl.when(pl.program_id(0) == 0)
    def _():
        zbuf[...] = jnp.zeros_like(zbuf)
        fills = [pltpu.make_async_copy(
            zbuf, hs_ref.at[pl.ds(pl.multiple_of(jnp.maximum(zrow_ref[e], 0), tg), tg)], zsem)
            for e in range(N_EXPERTS)]
        for e in range(N_EXPERTS):
            @pl.when(zrow_ref[e] >= 0)
            def _():
                fills[e].start()
        for e in range(N_EXPERTS):
            @pl.when(zrow_ref[e] >= 0)
            def _():
                fills[e].wait()

        def tail(t):
            return pltpu.make_async_copy(zbuf, hs_ref.at[pl.ds(pl.multiple_of(t * tg, tg), tg)], zsem)

        n_tiles = hs_ref.shape[0] // tg
        lax.fori_loop(zrow_ref[N_EXPERTS], n_tiles, lambda t, c: (tail(t).start(), c)[1], 0)
        lax.fori_loop(zrow_ref[N_EXPERTS], n_tiles, lambda t, c: (tail(t).wait(), c)[1], 0)

    def start(r, _):
        _row_copy(h_ref, r, hs_ref, d1_ref[0, r], sem).start()
        _row_copy(h_ref, r, hs_ref, d2_ref[0, r], sem).start(priority=1)
        return 0

    def wait(r, _):
        _row_copy(h_ref, r, hs_ref, d1_ref[0, r], sem).wait()
        _row_copy(h_ref, r, hs_ref, d2_ref[0, r], sem).wait()
        return 0

    lax.fori_loop(0, tm, start, 0, unroll=8)
    lax.fori_loop(0, tm, wait, 0, unroll=8)


def _moe_dispatch(h, dest1, dest2, zrow, rows, tm, tg):
    n, d = h.shape
    idx = pl.BlockSpec((None, 1, tm), lambda i, z: (i, 0, 0), memory_space=pltpu.SMEM)
    grid_spec = pltpu.PrefetchScalarGridSpec(
        num_scalar_prefetch=1,
        grid=(n // tm,),
        in_specs=[idx, idx, pl.BlockSpec((tm, d), lambda i, z: (i, 0))],
        out_specs=pl.BlockSpec(memory_space=pl.ANY),
        scratch_shapes=[pltpu.VMEM((tg, d), F32), pltpu.SemaphoreType.DMA, pltpu.SemaphoreType.DMA],
    )
    return pl.pallas_call(
        _moe_dispatch_kernel,
        out_shape=jax.ShapeDtypeStruct((rows, d), F32),
        grid_spec=grid_spec,
        compiler_params=_cparams(("arbitrary",)),
        name="moe_dispatch",
    )(zrow, dest1, dest2, h)


def _moe_experts_kernel(te_ref, nv_ref, x_ref, wg_ref, wu_ref, wd_ref, o_ref):
    del te_ref
    i = pl.program_id(0)

    @pl.when(i < nv_ref[0])
    def _():
        o_ref[...] = _swiglu_chunks(x_ref[...].astype(BF16), wg_ref, wu_ref, wd_ref, chunk=3 * MXU_TILE)

    @pl.when(i >= nv_ref[0])
    def _():
        o_ref[...] = jnp.zeros_like(o_ref)


def _moe_experts(hs, tile_expert, n_valid, wg, wu, wd, tg):
    rows, d = hs.shape
    ff = wg.shape[-1]
    grid_spec = pltpu.PrefetchScalarGridSpec(
        num_scalar_prefetch=2,
        grid=(rows // tg,),
        in_specs=[pl.BlockSpec((tg, d), lambda i, te, nv: (i, 0)),
                  pl.BlockSpec((None, d, ff), lambda i, te, nv: (te[i], 0, 0)),
                  pl.BlockSpec((None, d, ff), lambda i, te, nv: (te[i], 0, 0)),
                  pl.BlockSpec((None, ff, d), lambda i, te, nv: (te[i], 0, 0))],
        out_specs=pl.BlockSpec((tg, d), lambda i, te, nv: (i, 0)),
    )
    return pl.pallas_call(
        _moe_experts_kernel,
        out_shape=jax.ShapeDtypeStruct((rows, d), F32),
        grid_spec=grid_spec,
        compiler_params=_cparams(("arbitrary",)),
        name="moe_experts",
    )(tile_expert, n_valid, hs, wg, wu, wd)


def _moe_combine_kernel(d1_ref, d2_ref, x_ref, g_ref, route_ref, fw_ref, ys_ref, o_ref, buf1, buf2, sem,
                        *, final):
    tm = x_ref.shape[0]

    def start(r, _):
        _row_copy(ys_ref, d1_ref[0, r], buf1, r, sem).start()
        _row_copy(ys_ref, d2_ref[0, r], buf2, r, sem).start(priority=1)
        return 0

    def wait(r, _):
        _row_copy(ys_ref, d1_ref[0, r], buf1, r, sem).wait()
        _row_copy(ys_ref, d2_ref[0, r], buf2, r, sem).wait()
        return 0

    lax.fori_loop(0, tm, start, 0, unroll=8)
    lax.fori_loop(0, tm, wait, 0, unroll=8)
    route = route_ref[...]
    f = route[:, _R_G1:_R_G1 + 1] * buf1[...] + route[:, _R_G2:_R_G2 + 1] * buf2[...]
    y = x_ref[...] + g_ref[...] * f
    o_ref[...] = _final_norm(y, fw_ref[...]) if final else y


def _moe_combine(x, mod, l, route, dest1, dest2, ys, fw, tm, per_row_mod, final):
    b, s, d = x.shape
    nt = s // tm
    _, _, g = _mod_specs(l, d, tm, per_row_mod, 2)
    idx = pl.BlockSpec((None, 1, tm), lambda i, t: (i * nt + t, 0, 0), memory_space=pltpu.SMEM)
    tok = lambda w: pl.BlockSpec((None, tm, w), lambda i, t: (i, t, 0))
    return pl.pallas_call(
        functools.partial(_moe_combine_kernel, final=final),
        out_shape=jax.ShapeDtypeStruct((b, s, d), F32),
        grid=(b, nt),
        in_specs=[idx, idx, tok(d), g, tok(LANES), pl.BlockSpec((1, d), lambda i, t: (0, 0)),
                  pl.BlockSpec(memory_space=pl.ANY)],
        out_specs=tok(d),
        scratch_shapes=[pltpu.VMEM((tm, d), F32), pltpu.VMEM((tm, d), F32), pltpu.SemaphoreType.DMA],
        compiler_params=_cparams(("arbitrary", "arbitrary")),
        name="moe_combine",
    )(dest1, dest2, x, mod, route, fw, ys)


def _moe(x, mod, l, nw, wr, wg, wu, wd, fw, tm, tg, per_row_mod, final):
    b, s, d = x.shape
    n = b * s
    h, route, counts = _moe_route(x, mod, l, nw, wr, tm, per_row_mod)
    cnt = counts[0, :N_EXPERTS].astype(jnp.int32)
    padded = (cnt + tg - 1) // tg * tg
    ends = jnp.cumsum(padded)
    offs = ends - padded
    rows = (2 * n + N_EXPERTS * (tg - 1) + tg - 1) // tg * tg
    n_valid = ends[-1] // tg
    tile = jnp.minimum(jnp.arange(rows // tg, dtype=jnp.int32), n_valid - 1)
    tile_expert = jnp.minimum(jnp.sum(tile[:, None] * tg >= ends[None, :], axis=1), N_EXPERTS - 1).astype(jnp.int32)
    rt = route.reshape(n, LANES)
    dest = lambda ke, kr: (offs[rt[:, ke].astype(jnp.int32)] + rt[:, kr].astype(jnp.int32)).reshape(n // tm, 1, tm)
    dest1, dest2 = dest(_R_E1, _R_R1), dest(_R_E2, _R_R2)
    zrow = jnp.concatenate([jnp.where(padded > 0, ends - tg, -1), n_valid[None]]).astype(jnp.int32)
    hs = _moe_dispatch(h.reshape(n, d), dest1, dest2, zrow, rows, tm, tg)
    ys = _moe_experts(hs, tile_expert, n_valid.reshape(1), wg, wu, wd, tg)
    return _moe_combine(x, mod, l, route, dest1, dest2, ys, fw, tm, per_row_mod, final)


def _page_cumsum_kernel(lf_ref, o_ref):
    n = lf_ref.shape[-1]
    r = lax.broadcasted_iota(jnp.int32, (n, n), 0)
    c = lax.broadcasted_iota(jnp.int32, (n, n), 1)
    o_ref[...] = _dot_tri(lf_ref[...], (r <= c).astype(BF16))


def _page_cumsum(lf_t):
    depth, h, n_pool, page = lf_t.shape
    return pl.pallas_call(
        _page_cumsum_kernel,
        out_shape=jax.ShapeDtypeStruct(lf_t.shape, F32),
        grid=(depth, h),
        in_specs=[pl.BlockSpec((None, None, n_pool, page), lambda l, i: (l, i, 0, 0))],
        out_specs=pl.BlockSpec((None, None, n_pool, page), lambda l, i: (l, i, 0, 0)),
        compiler_params=_cparams(("arbitrary", "arbitrary")),
        name="page_cumsum",
    )(lf_t)


def _fox_decode_kernel(pt_ref, q_ref, kn_ref, vn_ref, lfn_ref, k_hbm, v_hbm, f_hbm, o_ref,
                       pbuf, fbuf, psem, fsem, s_sc, acc, qb, *, l, nbuf):
    b = pl.program_id(0)
    n_pages = pt_ref.shape[1]
    page = pbuf.shape[-1]
    per_seq = 2 * n_pages
    total = pl.num_programs(0) * per_seq

    def page_copy(src, seq, j, slot):
        return pltpu.make_async_copy(src.at[l, pt_ref[seq, j]], pbuf.at[slot], psem.at[slot])

    def f_copy(seq, j, slot):
        pg = pt_ref[seq, j]
        src = f_hbm.at[l, :, lax.shift_right_logical(pg, 3), pl.ds(pg & 7, 1), :]
        return pltpu.make_async_copy(src, fbuf.at[slot], fsem.at[slot])

    def start_item(g, slot):
        seq = lax.div(g, per_seq)
        i = g - seq * per_seq

        @pl.when(i < n_pages)
        def _():
            page_copy(k_hbm, seq, i, slot).start()
            f_copy(seq, i, slot).start()

        @pl.when(i >= n_pages)
        def _():
            page_copy(v_hbm, seq, i - n_pages, slot).start()

    @pl.when(b == 0)
    def _():
        for s in range(nbuf):
            page_copy(k_hbm, 0, s, s).start()
            f_copy(0, s, s).start()

    q3 = q_ref[...].reshape(HEADS, HEAD_DIM, 1) * (HEAD_DIM ** -0.5)
    qb[...] = jnp.broadcast_to(q3, qb.shape)

    def k_body(i, run):
        slot = i & (nbuf - 1)
        page_copy(k_hbm, b, i, slot).wait()
        f_copy(b, i, slot).wait()
        floc = fbuf[slot]
        sc = jnp.sum(pbuf[slot] * qb[...], axis=1, keepdims=True)
        s_sc[:, :, pl.ds(pl.multiple_of(i * page, page), page)] = sc - (run + floc)
        nxt = b * per_seq + i + nbuf

        @pl.when(nxt < total)
        def _():
            start_item(nxt, slot)

        return run + floc[:, :, page - 1:page]

    run = lax.fori_loop(0, n_pages, k_body, jnp.zeros((HEADS, 1, 1), F32))

    s_all = s_sc[...]
    s_new = jnp.sum(q3 * kn_ref[...].reshape(HEADS, HEAD_DIM, 1), axis=1, keepdims=True) - (run + lfn_ref[...])
    m = jnp.maximum(jnp.max(s_all, axis=2, keepdims=True), s_new)
    p = jnp.exp(s_all - m)
    p_new = jnp.exp(s_new - m)
    s_sc[...] = p
    denom = jnp.sum(p, axis=2, keepdims=True) + p_new
    acc[...] = jnp.zeros_like(acc)

    def v_body(i, _):
        slot = i & (nbuf - 1)
        page_copy(v_hbm, b, i, slot).wait()
        acc[...] += pbuf[slot] * s_sc[:, :, pl.ds(pl.multiple_of(i * page, page), page)]
        nxt = b * per_seq + n_pages + i + nbuf

        @pl.when(nxt < total)
        def _():
            start_item(nxt, slot)

        return 0

    lax.fori_loop(0, n_pages, v_body, 0)
    o3 = jnp.sum(acc[...], axis=2, keepdims=True) + p_new * vn_ref[...].reshape(HEADS, HEAD_DIM, 1)
    o_ref[...] = (o3 / denom).reshape(MIX_W, 1)


def _fox_decode(page_table, q_col, kn_col, vn_col, lfn, k_t, v_t, floc, l, nbuf):
    nb, n_pages = page_table.shape
    page = k_t.shape[-1]
    assert nbuf <= n_pages and nbuf & (nbuf - 1) == 0 and n_pages % nbuf == 0
    col = pl.BlockSpec((None, MIX_W, 1), lambda i, pt: (i, 0, 0))
    hbm = pl.BlockSpec(memory_space=pl.ANY)
    grid_spec = pltpu.PrefetchScalarGridSpec(
        num_scalar_prefetch=1,
        grid=(nb,),
        in_specs=[col, col, col, pl.BlockSpec((None, HEADS, 1, 1), lambda i, pt: (i, 0, 0, 0)), hbm, hbm, hbm],
        out_specs=col,
        scratch_shapes=[pltpu.VMEM((nbuf, HEADS, HEAD_DIM, page), F32),
                        pltpu.VMEM((nbuf, HEADS, 1, page), F32),
                        pltpu.SemaphoreType.DMA((nbuf,)),
                        pltpu.SemaphoreType.DMA((nbuf,)),
                        pltpu.VMEM((HEADS, 1, n_pages * page), F32),
                        pltpu.VMEM((HEADS, HEAD_DIM, page), F32),
                        pltpu.VMEM((HEADS, HEAD_DIM, page), F32)],
    )
    return pl.pallas_call(
        functools.partial(_fox_decode_kernel, l=l, nbuf=nbuf),
        out_shape=jax.ShapeDtypeStruct((nb, MIX_W, 1), F32),
        grid_spec=grid_spec,
        compiler_params=_cparams(("arbitrary",)),
        name="fox_decode",
    )(page_table, q_col, kn_col, vn_col, lfn, k_t, v_t, floc)


def _mlstm_decode_kernel(q_ref, k_ref, v_ref, og_ref, li_ref, lf_ref, c_ref, n_ref, m_ref, nw_ref,
                         o_ref, c_out, n_out, m_out):
    q = q_ref[...]
    k = k_ref[...] * (HEAD_DIM ** -0.5)
    v = v_ref[...]
    c = c_ref[...]
    n = n_ref[...]
    m = m_ref[...]
    li = li_ref[...]
    lf = lf_ref[...]
    inter = lf + m
    m_t = jnp.maximum(inter, li)
    a_int = jnp.exp(inter - m_t)
    w_in = jnp.exp(li - m_t)
    s = jnp.sum(q * k, axis=1, keepdims=True) * w_in
    num = a_int * jnp.sum(q * c, axis=1, keepdims=True) + s * v
    den = a_int * jnp.sum(q * n, axis=1, keepdims=True) + s
    h = num / jnp.maximum(jnp.abs(den), jnp.exp(-m_t))
    h = h * lax.rsqrt(jnp.mean(h * h, axis=-1, keepdims=True) + EPS)
    o_ref[...] = _sigmoid(og_ref[...]) * (h * nw_ref[...])
    c_out[...] = a_int * c + w_in * (k * v)
    n_out[...] = a_int * n + w_in * k
    m_out[...] = m_t


def _mlstm_decode(q_col, k_col, v_row, og_row, li, lf, c_all, l, n0_col, m0, nw_row):
    nb = q_col.shape[0]
    d = HEAD_DIM
    c0spec = pl.BlockSpec((None, None, HEADS, d, d), lambda i: (l, i, 0, 0, 0))
    colspec = pl.BlockSpec((None, HEADS, d, 1), lambda i: (i, 0, 0, 0))
    rowspec = pl.BlockSpec((None, HEADS, 1, d), lambda i: (i, 0, 0, 0))
    sclspec = pl.BlockSpec((None, HEADS, 1, 1), lambda i: (i, 0, 0, 0))
    matspec = pl.BlockSpec((None, HEADS, d, d), lambda i: (i, 0, 0, 0))
    return pl.pallas_call(
        _mlstm_decode_kernel,
        out_shape=(jax.ShapeDtypeStruct((nb, HEADS, 1, d), F32),
                   jax.ShapeDtypeStruct((nb, HEADS, d, d), F32),
                   jax.ShapeDtypeStruct((nb, HEADS, d, 1), F32),
                   jax.ShapeDtypeStruct((nb, HEADS, 1, 1), F32)),
        grid=(nb,),
        in_specs=[colspec, colspec, rowspec, rowspec, sclspec, sclspec, c0spec, colspec, sclspec,
                  pl.BlockSpec((HEADS, 1, d), lambda i: (0, 0, 0))],
        out_specs=(rowspec, matspec, colspec, sclspec),
        compiler_params=_cparams(("arbitrary",)),
        name="mlstm_decode",
    )(q_col, k_col, v_row, og_row, li, lf, c_all, n0_col, m0, nw_row)


def _pool_decode_kernel(buf_ref, u_ref, wbd_ref, ps_ref, o_ref, nb_ref, *, pos):
    u = u_ref[...]
    n = buf_ref.shape[0]
    lane = lax.broadcasted_iota(jnp.int32, u.shape, 1)
    sums = {}
    run = u
    for back in range(1, max(POOL_WINDOWS)):
        run = run + buf_ref[n - back]
        if back + 1 in POOL_WINDOWS:
            sums[back + 1] = run
    cnt = {w: float(min(pos + 1, w)) for w in POOL_WINDOWS}
    mean = jnp.where(lane < 64, sums[2] / cnt[2], jnp.where(lane < 128, sums[4] / cnt[4],
                     jnp.where(lane < 192, sums[8] / cnt[8], sums[16] / cnt[16])))
    o_ref[...] = _mm(mean - u, wbd_ref[...]) * ps_ref[...]
    for r in range(n - 1):
        nb_ref[r] = buf_ref[r + 1]
    nb_ref[n - 1] = u


def _pool_decode(buf_t, u, wbd, pscale, l, pos):
    _, n, nb, w = buf_t.shape
    return pl.pallas_call(
        functools.partial(_pool_decode_kernel, pos=pos),
        out_shape=(jax.ShapeDtypeStruct((nb, w), F32), jax.ShapeDtypeStruct((n, nb, w), F32)),
        grid=(1,),
        in_specs=[pl.BlockSpec((None, n, nb, w), lambda i: (l, 0, 0, 0)),
                  pl.BlockSpec((nb, w), lambda i: (0, 0)),
                  pl.BlockSpec((None, w, w), lambda i: (l, 0, 0)),
                  pl.BlockSpec((None, 1, w), lambda i: (l, 0, 0))],
        out_specs=(pl.BlockSpec((nb, w), lambda i: (0, 0)),
                   pl.BlockSpec((n, nb, w), lambda i: (0, 0, 0))),
        compiler_params=_cparams(("arbitrary",)),
        name="pool_decode",
    )(buf_t, u, wbd, pscale)


def _tile(n, pref):
    t = min(n, pref)
    assert n % t == 0, (n, t)
    return t


def kernel(x_prompt, x_sample, c_prompt, c_sample, cache_fox_k, cache_fox_v, cache_fox_lf, page_table,
           state_mlstm_C, state_mlstm_n, state_mlstm_m, state_pool, w_ada, b_ada, norm1_w, norm2_w,
           w_in, b_fox_f, b_mlstm_i, b_mlstm_f, mlstm_norm_w, w_pool, pool_scale, w_out,
           w_ffn_gate, w_ffn_up, w_ffn_down, w_router, w_exp_gate, w_exp_up, w_exp_down, final_norm_w):
    depth = w_in.shape[0]
    b, s, d = x_prompt.shape
    nb = x_sample.shape[0]
    n_pool, page = cache_fox_k.shape[1], cache_fox_k.shape[2]
    assert x_sample.shape[1] == 1 and s % MLSTM_CHUNK == 0 and n_pool % 8 == 0

    pad_gate = lambda w: jnp.pad(w, ((0, 0), (0, 0), (0, LANES - HEADS)))
    w_in_f = jnp.concatenate(
        [w_in[:, :, :_MAIN_W], pad_gate(w_in[:, :, _MAIN_W:_MAIN_W + HEADS]),
         pad_gate(w_in[:, :, _MAIN_W + HEADS:_MAIN_W + 2 * HEADS]),
         pad_gate(w_in[:, :, _MAIN_W + 2 * HEADS:])], axis=-1)
    w_in_p = w_in_f.astype(BF16)
    gate_b = jnp.pad(jnp.stack([b_fox_f, b_mlstm_i, b_mlstm_f], axis=1), ((0, 0), (0, 0), (0, LANES - HEADS)))
    wbd_f = jnp.zeros((depth, POOL_W, POOL_W), F32)
    for g in range(len(POOL_WINDOWS)):
        sl = slice(g * HEAD_DIM, (g + 1) * HEAD_DIM)
        wbd_f = wbd_f.at[:, sl, sl].set(w_pool[:, g])
    wbd = wbd_f.astype(BF16)
    pscale = pool_scale.reshape(depth, 1, POOL_W)
    w_out_b = w_out.astype(BF16)
    nw1 = norm1_w.reshape(depth, 1, d)
    nw2 = norm2_w.reshape(depth, 1, d)
    mnw = mlstm_norm_w.reshape(depth, 1, MIX_W)
    mnw_row = mlstm_norm_w.reshape(depth, HEADS, 1, HEAD_DIM)
    fw = final_norm_w.reshape(1, d)
    w_router_p = jnp.pad(w_router, ((0, 0), (0, 0), (0, LANES - N_EXPERTS)))
    wfg, wfu, wfd = (w.astype(BF16) for w in (w_ffn_gate, w_ffn_up, w_ffn_down))
    weg, weu, wed = (w.astype(BF16) for w in (w_exp_gate, w_exp_up, w_exp_down))

    k_t = jnp.transpose(cache_fox_k, (0, 1, 3, 4, 2))
    v_t = jnp.transpose(cache_fox_v, (0, 1, 3, 4, 2))
    lf_t = jnp.transpose(cache_fox_lf, (0, 3, 1, 2))
    floc = _page_cumsum(lf_t).reshape(depth, HEADS, n_pool // 8, 8, page)
    pool_t = jnp.transpose(state_pool, (0, 2, 1, 3))

    mod = _ada_modulation(jnp.concatenate([c_prompt, c_sample], axis=0), w_ada, b_ada)
    mod_p = mod[:, :, :b].reshape(depth, 6, b, 1, d)
    mod_s = mod[:, :, b:]

    tm = _tile(s, 512)
    tq = _tile(s, 1024)
    tf = w_ffn_gate.shape[-1]
    xp = x_prompt
    xs = x_sample.reshape(1, nb, d)
    st_p, st_s = [], []
    for l in range(depth):
        last = l == depth - 1
        (qa, ka, va, kf, vf, lff, qm, km, vm, om, li, lfm, opool, ulast) = _in_proj_prompt(
            xp, mod_p, l, nw1, w_in_p, gate_b, wbd, pscale, tm)
        o_fox = _fox_prompt(qa, ka, va, tq)
        o_m, c_pair, n_pair, m_fin = _mlstm_prompt(qm, km, vm, om, li, lfm, mnw, l, 4 if b % 4 == 0 else 1)
        x1 = _out_proj(o_fox, o_m, opool, xp, mod_p, l, w_out_b, tm, False)
        if l % 2 == 0:
            j = l // 2
            xp = _ffn(x1, mod_p, l, nw2, wfg[j], wfu[j], wfd[j], fw, tm, tf, False, last)
        else:
            j = l // 2
            xp = _moe(x1, mod_p, l, nw2, w_router_p[j], weg[j], weu[j], wed[j], fw, tm, 256, False, last)
        c_fin = jnp.stack([c_pair[:, p, e * HEAD_DIM:(e + 1) * HEAD_DIM, e * HEAD_DIM:(e + 1) * HEAD_DIM]
                           for p in range(HEADS // 2) for e in range(2)], axis=1)
        st_p.append((kf.reshape(b, s, HEADS, HEAD_DIM), vf.reshape(b, s, HEADS, HEAD_DIM), lff[:, :, :HEADS],
                     c_fin, n_pair.reshape(b, HEADS, HEAD_DIM), m_fin[:, 0, :HEADS],
                     ulast[:, POOL_HALO - POOL_BUF:]))

        z, lff_s, li_s, lfm_s = _in_proj_sample(xs[0], mod_s, l, nw1, w_in_f, gate_b)
        seg = lambda i: z[:, i * MIX_W:(i + 1) * MIX_W]
        qf_s, kf_s, vf_s, qm_s, km_s, vm_s, om_s = (seg(i) for i in range(7))
        u_s = z[:, 7 * MIX_W:]
        col = lambda a: a.reshape(nb, MIX_W, 1)
        o_fox_s = _fox_decode(page_table, col(qf_s), col(kf_s), col(vf_s),
                              lff_s[:, :HEADS].reshape(nb, HEADS, 1, 1), k_t, v_t, floc, l,
                              min(32, page_table.shape[1]))
        hcol = lambda a: a.reshape(nb, HEADS, HEAD_DIM, 1)
        hrow = lambda a: a.reshape(nb, HEADS, 1, HEAD_DIM)
        hscl = lambda a: a[:, :HEADS].reshape(nb, HEADS, 1, 1)
        o_m_s, c_new, n_new, m_new = _mlstm_decode(
            hcol(qm_s), hcol(km_s), hrow(vm_s), hrow(om_s), hscl(li_s), hscl(lfm_s),
            state_mlstm_C, l, hcol(state_mlstm_n[l]), state_mlstm_m[l].reshape(nb, HEADS, 1, 1), mnw_row[l])
        o_pool_s, buf_new = _pool_decode(pool_t, u_s, wbd_f, pscale, l, page_table.shape[1] * page)
        mix = lambda a: a.reshape(1, nb, -1)
        x1s = _out_proj(mix(o_fox_s), mix(o_m_s), mix(o_pool_s), xs, mod_s, l, w_out, nb, True)
        if l % 2 == 0:
            xs = _ffn(x1s, mod_s, l, nw2, w_ffn_gate[j], w_ffn_up[j], w_ffn_down[j], fw, nb, 2 * LANES, True, last)
        else:
            xs = _moe(x1s, mod_s, l, nw2, w_router_p[j], weg[j], weu[j], wed[j], fw, nb, 8, True, last)
        st_s.append((kf_s.reshape(nb, 1, HEADS, HEAD_DIM), vf_s.reshape(nb, 1, HEADS, HEAD_DIM),
                     lff_s[:, :HEADS].reshape(nb, 1, HEADS), c_new, n_new.reshape(nb, HEADS, HEAD_DIM),
                     m_new.reshape(nb, HEADS), jnp.transpose(buf_new, (1, 0, 2))))

    outs_p = [jnp.stack(a) for a in zip(*st_p)]
    outs_s = [jnp.stack(a) for a in zip(*st_s)]
    return (xp, xs.reshape(nb, 1, d), *outs_p, *outs_s)
```

```python
import functools

import jax
import jax.numpy as jnp
from jax import lax
from jax.experimental import pallas as pl
from jax.experimental.pallas import tpu as pltpu

F32 = jnp.float32
BF16 = jnp.bfloat16

HEAD_DIM = 64
LANES = 128
MXU_TILE = 256
HEADS = 6
MIX_W = HEADS * HEAD_DIM
POOL_WINDOWS = (2, 4, 8, 16)
POOL_W = 256
POOL_HALO = 16
POOL_BUF = 15
MLSTM_CHUNK = 128
N_EXPERTS = 8
EPS = 1e-6
NEG_INF = float("-inf")

_MAIN_W = 7 * MIX_W + POOL_W
_GATE_OFF = _MAIN_W
IN_PAD_W = _MAIN_W + 3 * LANES

VMEM_LIMIT = 56 * 1024 * 1024


def _cparams(sem):
    return pltpu.CompilerParams(dimension_semantics=sem, vmem_limit_bytes=VMEM_LIMIT)


def _sigmoid(x):
    return 1.0 / (1.0 + jnp.exp(-x))


def _log_sigmoid(x):
    return jnp.minimum(x, 0.0) - jnp.log(1.0 + jnp.exp(-jnp.abs(x)))


def _split3(x):
    hi = x.astype(BF16).astype(F32)
    r = x - hi
    mid = r.astype(BF16).astype(F32)
    lo = (r - mid).astype(BF16).astype(F32)
    return hi, mid, lo


def _dot(a, b):
    return jnp.dot(a, b, preferred_element_type=F32)


def _dot_nt(a, b):
    return lax.dot_general(a, b, (((1,), (1,)), ((), ())), preferred_element_type=F32)


def _dot_tn(a, b):
    return lax.dot_general(a, b, (((0,), (0,)), ((), ())), preferred_element_type=F32)


def _tri_dot(tri_bf16, x):
    hi, mid, lo = _split3(x)
    return (_dot(tri_bf16, hi.astype(BF16)) + _dot(tri_bf16, mid.astype(BF16))
            + _dot(tri_bf16, lo.astype(BF16)))


def _dot_tri(x, tri_bf16):
    hi, mid, lo = _split3(x)
    return (_dot(hi.astype(BF16), tri_bf16) + _dot(mid.astype(BF16), tri_bf16)
            + _dot(lo.astype(BF16), tri_bf16))


def _dot_f32(a, b, passes=6):
    a0, a1, a2 = (p.astype(BF16) for p in _split3(a))
    b0, b1, b2 = (p.astype(BF16) for p in _split3(b))
    out = _dot(a0, b1) + _dot(a1, b0)
    if passes == 6:
        out = (_dot(a0, b2) + _dot(a2, b0) + _dot(a1, b1)) + out
    return out + _dot(a0, b0)


def _mm(a, w):
    if w.dtype == F32:
        return _dot_f32(a.astype(F32), w)
    return _dot(a.astype(BF16), w)


def _lower_tri(n):
    r = lax.broadcasted_iota(jnp.int32, (n, n), 0)
    c = lax.broadcasted_iota(jnp.int32, (n, n), 1)
    return r >= c


def _modulated_rmsnorm(x, nw, sc, sh):
    y = x * lax.rsqrt(jnp.mean(x * x, axis=-1, keepdims=True) + EPS)
    return (y * nw) * (1.0 + sc) + sh


def _ada_kernel(c_ref, w_ref, b_ref, o_ref):
    c = c_ref[...]
    s = c * _sigmoid(c)
    o_ref[...] = _mm(s, w_ref[...]) + b_ref[...]


def _ada_modulation(c_all, w_ada, b_ada):
    depth, d, _ = w_ada.shape
    r = c_all.shape[0]
    return pl.pallas_call(
        _ada_kernel,
        out_shape=jax.ShapeDtypeStruct((depth, 6, r, d), F32),
        grid=(depth, 6),
        in_specs=[
            pl.BlockSpec((r, d), lambda l, j: (0, 0)),
            pl.BlockSpec((None, d, d), lambda l, j: (l, 0, j)),
            pl.BlockSpec((None, 1, d), lambda l, j: (l, 0, j)),
        ],
        out_specs=pl.BlockSpec((None, None, r, d), lambda l, j: (l, j, 0, 0)),
        compiler_params=_cparams(("arbitrary", "arbitrary")),
        name="ada_modulation",
    )(c_all, w_ada, b_ada.reshape(depth, 1, 6 * d))


def _pool_mix(u, halo, pos0, wbd, scale):
    t = u.shape[0]
    ext = jnp.concatenate([halo, u], axis=0)
    p2 = ext + pltpu.roll(ext, 1, 0)
    p4 = p2 + pltpu.roll(p2, 2, 0)
    p8 = p4 + pltpu.roll(p4, 4, 0)
    p16 = p8 + pltpu.roll(p8, 8, 0)
    lane = lax.broadcasted_iota(jnp.int32, (t, POOL_W), 1)
    sums = jnp.where(lane < 64, p2[POOL_HALO:], jnp.where(lane < 128, p4[POOL_HALO:],
                     jnp.where(lane < 192, p8[POOL_HALO:], p16[POOL_HALO:])))
    win = jnp.where(lane < 64, 2.0, jnp.where(lane < 128, 4.0, jnp.where(lane < 192, 8.0, 16.0)))
    pos = (pos0 + lax.broadcasted_iota(jnp.int32, (t, POOL_W), 0)).astype(F32)
    cnt = jnp.minimum(pos + 1.0, win)
    d = sums / cnt - u
    return _dot(d.astype(BF16), wbd) * scale


def _in_proj_prompt_kernel(x_ref, nw_ref, sc_ref, sh_ref, w_ref, gb_ref, wbd_ref, ps_ref,
                           qa_ref, ka_ref, va_ref, kf_ref, vf_ref, lff_ref,
                           qm_ref, km_ref, vm_ref, om_ref, li_ref, lfm_ref, op_ref, ul_ref,
                           fcarry, uhalo):
    t = pl.program_id(1)
    tm = x_ref.shape[0]

    @pl.when(t == 0)
    def _():
        fcarry[...] = jnp.zeros_like(fcarry)
        uhalo[...] = jnp.zeros_like(uhalo)

    hb = _modulated_rmsnorm(x_ref[...], nw_ref[...], sc_ref[...], sh_ref[...]).astype(BF16)
    split = 4 * MIX_W
    za = _dot(hb, w_ref[:, 0:split])
    zb = _dot(hb, w_ref[:, split:IN_PAD_W])
    piece_a = lambda i: za[:, i * MIX_W:(i + 1) * MIX_W]
    piece_b = lambda i: zb[:, i * MIX_W:(i + 1) * MIX_W]

    lane = lax.broadcasted_iota(jnp.int32, (tm, LANES), 1)
    zg = zb[:, _GATE_OFF - split:IN_PAD_W - split]
    head_lane = lane < HEADS
    lf_fox = jnp.where(head_lane, _log_sigmoid(zg[:, 0:LANES] + gb_ref[0:1, :]), 0.0)
    li = jnp.where(head_lane, zg[:, LANES:2 * LANES] + gb_ref[1:2, :], 0.0)
    lf_m = jnp.where(head_lane, _log_sigmoid(zg[:, 2 * LANES:3 * LANES] + gb_ref[2:3, :]), 0.0)
    lff_ref[...] = lf_fox
    li_ref[...] = li
    lfm_ref[...] = lf_m

    cum = _tri_dot(_lower_tri(tm).astype(BF16), lf_fox) + fcarry[...]
    fcarry[...] = cum[tm - 1:tm, :]

    zq = piece_a(0) * (HEAD_DIM ** -0.5)
    zk = piece_a(1)
    zv = piece_a(2)
    kf_ref[...] = zk
    vf_ref[...] = zv
    low = lane < HEAD_DIM
    for h in range(HEADS):
        p, e = divmod(h, 2)
        sl = slice(p * LANES, (p + 1) * LANES)
        bq, bk, bv = zq[:, sl], zk[:, sl], zv[:, sl]
        if e == 1:
            bq, bk, bv = (pltpu.roll(a, HEAD_DIM, 1) for a in (bq, bk, bv))
        hi, mid, lo = _split3(cum[:, h:h + 1])
        augq = jnp.where(lane == 64, hi, jnp.where(lane == 65, mid, jnp.where(lane == 66, lo,
                         jnp.where(lane < 70, 1.0, 0.0))))
        augk = jnp.where(lane < 67, 1.0, jnp.where(lane == 67, -hi, jnp.where(lane == 68, -mid,
                         jnp.where(lane == 69, -lo, 0.0))))
        augv = jnp.where(lane == 64, 1.0, 0.0)
        qa_ref[h] = jnp.where(low, bq, augq).astype(BF16)
        ka_ref[h] = jnp.where(low, bk, augk).astype(BF16)
        va_ref[h] = jnp.where(low, bv, augv).astype(BF16)

    qm_ref[...] = piece_a(3).astype(BF16)
    km_ref[...] = (piece_b(0) * (HEAD_DIM ** -0.5)).astype(BF16)
    vm_ref[...] = piece_b(1).astype(BF16)
    om_ref[...] = piece_b(2)

    u = zb[:, 3 * MIX_W:3 * MIX_W + POOL_W]
    op_ref[...] = _pool_mix(u, uhalo[...], t * tm, wbd_ref[...], ps_ref[...]).astype(BF16)
    uhalo[...] = u[tm - POOL_HALO:, :]
    ul_ref[...] = u[tm - POOL_HALO:, :]


def _in_proj_prompt(x, mod, l, nw, w_in_p, gate_b, wbd, pscale, tm):
    b, s, d = x.shape
    nt = s // tm
    row = lambda width, dt: jax.ShapeDtypeStruct((b, s, width), dt)
    head = jax.ShapeDtypeStruct((b, HEADS, s, LANES), BF16)
    tok = lambda width: pl.BlockSpec((None, tm, width), lambda i, t: (i, t, 0))
    hspec = pl.BlockSpec((None, HEADS, tm, LANES), lambda i, t: (i, 0, t, 0))
    modspec = lambda j: pl.BlockSpec((None, None, None, 1, d), lambda i, t: (l, j, i, 0, 0))
    return pl.pallas_call(
        _in_proj_prompt_kernel,
        out_shape=(head, head, head, row(MIX_W, F32), row(MIX_W, F32), row(LANES, F32),
                   row(MIX_W, BF16), row(MIX_W, BF16), row(MIX_W, BF16), row(MIX_W, F32),
                   row(LANES, F32), row(LANES, F32), row(POOL_W, BF16),
                   jax.ShapeDtypeStruct((b, POOL_HALO, POOL_W), F32)),
        grid=(b, nt),
        in_specs=[
            tok(d),
            pl.BlockSpec((None, 1, d), lambda i, t: (l, 0, 0)),
            modspec(1), modspec(0),
            pl.BlockSpec((None, d, IN_PAD_W), lambda i, t: (l, 0, 0)),
            pl.BlockSpec((None, 3, LANES), lambda i, t: (l, 0, 0)),
            pl.BlockSpec((None, POOL_W, POOL_W), lambda i, t: (l, 0, 0)),
            pl.BlockSpec((None, 1, POOL_W), lambda i, t: (l, 0, 0)),
        ],
        out_specs=(hspec, hspec, hspec, tok(MIX_W), tok(MIX_W), tok(LANES),
                   tok(MIX_W), tok(MIX_W), tok(MIX_W), tok(MIX_W), tok(LANES), tok(LANES),
                   tok(POOL_W), pl.BlockSpec((None, POOL_HALO, POOL_W), lambda i, t: (i, 0, 0))),
        scratch_shapes=[pltpu.VMEM((1, LANES), F32), pltpu.VMEM((POOL_HALO, POOL_W), F32)],
        compiler_params=_cparams(("arbitrary", "arbitrary")),
        name="in_proj_prompt",
    )(x, nw, mod, mod, w_in_p, gate_b, wbd, pscale)


def _in_proj_sample_kernel(x_ref, nw_ref, sc_ref, sh_ref, w_ref, gb_ref, z_ref, lff_ref, li_ref, lfm_ref):
    h = _modulated_rmsnorm(x_ref[...], nw_ref[...], sc_ref[...], sh_ref[...])
    chunk = 4 * LANES
    for c0 in range(0, _MAIN_W, chunk):
        c1 = min(c0 + chunk, _MAIN_W)
        z_ref[:, c0:c1] = _mm(h, w_ref[:, c0:c1])
    zg = _mm(h, w_ref[:, _GATE_OFF:IN_PAD_W])
    lff_ref[...] = _log_sigmoid(zg[:, 0:LANES] + gb_ref[0:1, :])
    li_ref[...] = zg[:, LANES:2 * LANES] + gb_ref[1:2, :]
    lfm_ref[...] = _log_sigmoid(zg[:, 2 * LANES:3 * LANES] + gb_ref[2:3, :])


def _in_proj_sample(x, mod, l, nw, w_in_p, gate_b):
    n, d = x.shape
    gate = jax.ShapeDtypeStruct((n, LANES), F32)
    full = lambda *shape: pl.BlockSpec(shape, lambda i: (0,) * len(shape))
    modspec = lambda j: pl.BlockSpec((None, None, n, d), lambda i: (l, j, 0, 0))
    return pl.pallas_call(
        _in_proj_sample_kernel,
        out_shape=(jax.ShapeDtypeStruct((n, _MAIN_W), F32), gate, gate, gate),
        grid=(1,),
        in_specs=[
            full(n, d),
            pl.BlockSpec((None, 1, d), lambda i: (l, 0, 0)),
            modspec(1), modspec(0),
            pl.BlockSpec((None, d, IN_PAD_W), lambda i: (l, 0, 0)),
            pl.BlockSpec((None, 3, LANES), lambda i: (l, 0, 0)),
        ],
        out_specs=(full(n, _MAIN_W), full(n, LANES), full(n, LANES), full(n, LANES)),
        compiler_params=_cparams(("arbitrary",)),
        name="in_proj_sample",
    )(x, nw, mod, mod, w_in_p, gate_b)


def _fox_prompt_kernel(qa_ref, ka_ref, va_ref, o_ref, *, tq):
    qi = pl.program_id(2)
    q = (qa_ref[0], qa_ref[1])

    def update(qe, e, start, size, carry, causal):
        m, acc = carry
        k = ka_ref[e, pl.ds(start, size), :]
        v = va_ref[e, pl.ds(start, size), :]
        s = _dot_nt(qe, k)
        if causal:
            r = lax.broadcasted_iota(jnp.int32, s.shape, 0)
            c = lax.broadcasted_iota(jnp.int32, s.shape, 1)
            s = jnp.where(c <= r, s, NEG_INF)
        m_new = jnp.maximum(m, jnp.max(s, axis=-1, keepdims=True))
        p = jnp.exp(s - m_new)
        return m_new, acc * jnp.exp(m - m_new) + _dot(p.astype(BF16), v)

    def body(ki, carry):
        start = pl.multiple_of(ki * tq, tq)
        return tuple(update(q[e], e, start, tq, carry[e], False) for e in range(2))

    init = (jnp.full((tq, 1), NEG_INF, F32), jnp.zeros((tq, LANES), F32))
    carry = lax.fori_loop(0, qi, body, (init, init))
    half = tq // 2
    base = pl.multiple_of(qi * tq, tq)
    outs = []
    for e in range(2):
        m, acc = update(q[e], e, base, half, carry[e], True)
        _, acc_lo = update(q[e][half:], e, base + half, half, (m[half:], acc[half:]), True)
        acc = jnp.concatenate([acc[:half], acc_lo], axis=0)
        outs.append(acc / acc[:, HEAD_DIM:HEAD_DIM + 1])
    lane = lax.broadcasted_iota(jnp.int32, (tq, LANES), 1)
    o_ref[...] = jnp.where(lane < HEAD_DIM, outs[0], pltpu.roll(outs[1], HEAD_DIM, 1)).astype(o_ref.dtype)


def _fox_prompt(qa, ka, va, tq):
    b, _, s, _ = qa.shape
    return pl.pallas_call(
        functools.partial(_fox_prompt_kernel, tq=tq),
        out_shape=jax.ShapeDtypeStruct((b, s, MIX_W), BF16),
        grid=(b, HEADS // 2, s // tq),
        in_specs=[
            pl.BlockSpec((None, 2, tq, LANES), lambda i, p, t: (i, p, t, 0)),
            pl.BlockSpec((None, 2, s, LANES), lambda i, p, t: (i, p, 0, 0)),
            pl.BlockSpec((None, 2, s, LANES), lambda i, p, t: (i, p, 0, 0)),
        ],
        out_specs=pl.BlockSpec((None, tq, LANES), lambda i, p, t: (i, t, p)),
        compiler_params=_cparams(("arbitrary", "arbitrary", "arbitrary")),
        name="fox_prompt",
    )(qa, ka, va)


def _mlstm_prompt_kernel(q_ref, k_ref, v_ref, og_ref, li_ref, lf_ref, nw_ref,
                         o_ref, c_out, n_out, m_out, c_st, n_st, m_st):
    c = pl.program_id(1)
    nb, L, _ = q_ref.shape

    @pl.when(c == 0)
    def _():
        c_st[...] = jnp.zeros_like(c_st)
        n_st[...] = jnp.zeros_like(n_st)
        m_st[...] = jnp.zeros_like(m_st)

    tril = _lower_tri(L)
    tri = tril.astype(BF16)
    lane = lax.broadcasted_iota(jnp.int32, (L, LANES), 1)
    lane1 = lax.broadcasted_iota(jnp.int32, (1, LANES), 1)
    rowi = lax.broadcasted_iota(jnp.int32, (LANES, LANES), 0)
    coli = lax.broadcasted_iota(jnp.int32, (LANES, LANES), 1)
    same_head = (rowi < HEAD_DIM) == (coli < HEAD_DIM)

    for bi in range(nb):
        lf = lf_ref[bi]
        li = li_ref[bi]
        bcum = _tri_dot(tri, lf)
        m_prev = m_st[bi]
        b_end = bcum[L - 1:L, :]
        g = b_end - bcum + li
        m_new = jnp.maximum(b_end + m_prev, jnp.max(g, axis=0, keepdims=True))
        a_state = jnp.exp(b_end + m_prev - m_new)
        wg = jnp.exp(g - m_new)
        inter = bcum + m_prev
        bcum_t = bcum.T
        li_t = li.T
        heads = range(HEADS)
        pairs = [slice(p * LANES, (p + 1) * LANES) for p in range(HEADS // 2)]
        qs = [q_ref[bi, :, sl] for sl in pairs]
        ks = [k_ref[bi, :, sl] for sl in pairs]
        vs = [v_ref[bi, :, sl] for sl in pairs]
        mine = [(lane < HEAD_DIM) if h % 2 == 0 else (lane >= HEAD_DIM) for h in heads]
        own = jnp.stack(mine)
        q6 = jnp.stack([jnp.where(mine[h], qs[h // 2], jnp.zeros_like(qs[0])) for h in heads])
        k6 = jnp.stack([ks[h // 2] for h in heads])
        v6 = jnp.stack([vs[h // 2] for h in heads])
        n6 = jnp.stack([n_st[bi, h // 2] for h in heads])
        qc3 = [_dot(qs[p], c_st[bi, p].astype(BF16)) for p in range(HEADS // 2)]
        qc6 = jnp.stack([qc3[h // 2] for h in heads])
        bcol = jnp.stack([bcum[:, h:h + 1] for h in heads])
        icol = jnp.stack([inter[:, h:h + 1] for h in heads])
        brow = jnp.stack([bcum_t[h:h + 1, :] for h in heads])
        lrow = jnp.stack([li_t[h:h + 1, :] for h in heads])
        dmat = jnp.where(tril[None], bcol - brow + lrow, NEG_INF)
        m_t = jnp.maximum(icol, jnp.max(dmat, axis=-1, keepdims=True))
        a_int = jnp.exp(icol - m_t)
        s = lax.dot_general(q6, k6, (((2,), (2,)), ((0,), (0,))), preferred_element_type=F32) \
            * jnp.exp(dmat - m_t)
        sv = lax.dot_general(s.astype(BF16), v6, (((2,), (1,)), ((0,), (0,))), preferred_element_type=F32)
        den = a_int * jnp.sum(q6.astype(F32) * n6, axis=-1, keepdims=True) + jnp.sum(s, axis=-1, keepdims=True)
        hh = (a_int * qc6 + sv) / jnp.maximum(jnp.abs(den), jnp.exp(-m_t))
        ms = jnp.sum(jnp.where(own, hh * hh, 0.0), axis=-1, keepdims=True) * (1.0 / HEAD_DIM)
        hn = hh * lax.rsqrt(ms + EPS)
        for p in range(HEADS // 2):
            sl = pairs[p]
            k, v = ks[p], vs[p]
            cp = c_st[bi, p]
            npair = n_st[bi, p]
            hpair = jnp.where(lane < HEAD_DIM, hn[2 * p], hn[2 * p + 1])
            o_ref[bi, :, sl] = (_sigmoid(og_ref[bi, :, sl]) * (hpair * nw_ref[:, sl])).astype(o_ref.dtype)
            wgp = jnp.where(lane < HEAD_DIM, wg[:, 2 * p:2 * p + 1], wg[:, 2 * p + 1:2 * p + 2])
            ap = jnp.where(lane1 < HEAD_DIM, a_state[:, 2 * p:2 * p + 1], a_state[:, 2 * p + 1:2 * p + 2])
            kv = _dot_tn(k, (v.astype(F32) * wgp).astype(BF16))
            c_st[bi, p] = cp * ap + jnp.where(same_head, kv, 0.0)
            n_st[bi, p] = npair * ap + jnp.sum(k.astype(F32) * wgp, axis=0, keepdims=True)
        m_st[bi] = m_new

    @pl.when(c == pl.num_programs(1) - 1)
    def _():
        c_out[...] = c_st[...]
        n_out[...] = n_st[...]
        m_out[...] = m_st[...]


def _mlstm_prompt(qm, km, vm, om, li, lfm, nw, l, nb):
    b, s, _ = qm.shape
    L = MLSTM_CHUNK
    np_ = HEADS // 2
    tok = lambda w: pl.BlockSpec((nb, L, w), lambda i, c: (i, c, 0))
    return pl.pallas_call(
        _mlstm_prompt_kernel,
        out_shape=(jax.ShapeDtypeStruct((b, s, MIX_W), BF16),
                   jax.ShapeDtypeStruct((b, np_, LANES, LANES), F32),
                   jax.ShapeDtypeStruct((b, np_, 1, LANES), F32),
                   jax.ShapeDtypeStruct((b, 1, LANES), F32)),
        grid=(b // nb, s // L),
        in_specs=[tok(MIX_W), tok(MIX_W), tok(MIX_W), tok(MIX_W), tok(LANES), tok(LANES),
                  pl.BlockSpec((None, 1, MIX_W), lambda i, c: (l, 0, 0))],
        out_specs=(tok(MIX_W),
                   pl.BlockSpec((nb, np_, LANES, LANES), lambda i, c: (i, 0, 0, 0)),
                   pl.BlockSpec((nb, np_, 1, LANES), lambda i, c: (i, 0, 0, 0)),
                   pl.BlockSpec((nb, 1, LANES), lambda i, c: (i, 0, 0))),
        scratch_shapes=[pltpu.VMEM((nb, np_, LANES, LANES), F32),
                        pltpu.VMEM((nb, np_, 1, LANES), F32),
                        pltpu.VMEM((nb, 1, LANES), F32)],
        compiler_params=_cparams(("arbitrary", "arbitrary")),
        name="mlstm_prompt",
    )(qm, km, vm, om, li, lfm, nw)


def _out_proj_kernel(of_ref, om_ref, op_ref, x_ref, g_ref, w_ref, o_ref):
    y = _mm(jnp.concatenate([of_ref[...], om_ref[...], op_ref[...]], axis=1), w_ref[...])
    o_ref[...] = x_ref[...] + g_ref[...] * y


def _out_proj(of, om, op, x, mod, l, w_out_b, tm, per_row_mod):
    b, s, d = x.shape
    tok = lambda w: pl.BlockSpec((None, tm, w), lambda i, t: (i, t, 0))
    if per_row_mod:
        gspec = pl.BlockSpec((None, None, tm, d), lambda i, t: (l, 2, 0, 0))
    else:
        gspec = pl.BlockSpec((None, None, None, 1, d), lambda i, t: (l, 2, i, 0, 0))
    return pl.pallas_call(
        _out_proj_kernel,
        out_shape=jax.ShapeDtypeStruct((b, s, d), F32),
        grid=(b, s // tm),
        in_specs=[tok(MIX_W), tok(MIX_W), tok(POOL_W), tok(d), gspec,
                  pl.BlockSpec((None, d, d), lambda i, t: (l, 0, 0))],
        out_specs=tok(d),
        compiler_params=_cparams(("arbitrary", "arbitrary")),
        name="out_proj",
    )(of, om, op, x, mod, w_out_b)


def _final_norm(x, fw):
    return (x * lax.rsqrt(jnp.mean(x * x, axis=-1, keepdims=True) + EPS)) * fw


def _swiglu_chunks(h, wg_ref, wu_ref, wd_ref, chunk=MXU_TILE):
    ff = wg_ref.shape[-1]
    y = None
    for c0 in range(0, ff, chunk):
        c1 = min(c0 + chunk, ff)
        a = _mm(h, wg_ref[:, c0:c1])
        u = _mm(h, wu_ref[:, c0:c1])
        part = _mm((a * _sigmoid(a)) * u, wd_ref[c0:c1, :])
        y = part if y is None else y + part
    return y


def _ffn_kernel(x_ref, nw_ref, sc_ref, sh_ref, g_ref, wg_ref, wu_ref, wd_ref, fw_ref, o_ref,
                h_sc, acc, *, final):
    j = pl.program_id(2)

    @pl.when(j == 0)
    def _():
        h_sc[...] = _modulated_rmsnorm(x_ref[...], nw_ref[...], sc_ref[...], sh_ref[...]).astype(h_sc.dtype)
        acc[...] = jnp.zeros_like(acc)

    h = h_sc[...]
    acc[...] += _swiglu_chunks(h, wg_ref, wu_ref, wd_ref)

    @pl.when(j == pl.num_programs(2) - 1)
    def _():
        y = x_ref[...] + g_ref[...] * acc[...]
        o_ref[...] = _final_norm(y, fw_ref[...]) if final else y


def _mod_specs(l, d, tm, per_row_mod, nidx):
    def spec(j):
        if per_row_mod:
            return pl.BlockSpec((None, None, tm, d), lambda *a: (l, j, 0, 0))
        return pl.BlockSpec((None, None, None, 1, d), lambda *a: (l, j, a[0], 0, 0))
    return spec(4), spec(3), spec(5)


def _ffn(x, mod, l, nw, wg, wu, wd, fw, tm, tf, per_row_mod, final):
    b, s, d = x.shape
    ff = wg.shape[-1]
    sc, sh, g = _mod_specs(l, d, tm, per_row_mod, 3)
    return pl.pallas_call(
        functools.partial(_ffn_kernel, final=final),
        out_shape=jax.ShapeDtypeStruct((b, s, d), F32),
        grid=(b, s // tm, ff // tf),
        in_specs=[
            pl.BlockSpec((None, tm, d), lambda i, t, j: (i, t, 0)),
            pl.BlockSpec((None, 1, d), lambda i, t, j: (l, 0, 0)),
            sc, sh, g,
            pl.BlockSpec((d, tf), lambda i, t, j: (0, j)),
            pl.BlockSpec((d, tf), lambda i, t, j: (0, j)),
            pl.BlockSpec((tf, d), lambda i, t, j: (j, 0)),
            pl.BlockSpec((1, d), lambda i, t, j: (0, 0)),
        ],
        out_specs=pl.BlockSpec((None, tm, d), lambda i, t, j: (i, t, 0)),
        scratch_shapes=[pltpu.VMEM((tm, d), wg.dtype), pltpu.VMEM((tm, d), F32)],
        compiler_params=_cparams(("arbitrary", "arbitrary", "arbitrary")),
        name="ffn_swiglu",
    )(x, nw, mod, mod, mod, wg, wu, wd, fw)


def _top2(logits):
    lane = lax.broadcasted_iota(jnp.int32, logits.shape, 1)
    valid = lane < N_EXPERTS
    z = jnp.where(valid, logits, NEG_INF)
    pz = jnp.exp(z - jnp.max(z, axis=-1, keepdims=True))
    probs = pz / jnp.sum(pz, axis=-1, keepdims=True)
    m1 = jnp.max(probs, axis=-1, keepdims=True)
    i1 = jnp.min(jnp.where(probs == m1, lane, LANES), axis=-1, keepdims=True)
    rest = jnp.where((lane == i1) | (~valid), -1.0, probs)
    m2 = jnp.max(rest, axis=-1, keepdims=True)
    i2 = jnp.min(jnp.where(rest == m2, lane, LANES), axis=-1, keepdims=True)
    tot = m1 + m2
    return i1, i2, m1 / tot, m2 / tot


_R_E1, _R_E2, _R_R1, _R_R2, _R_G1, _R_G2 = range(6)


def _moe_route_kernel(x_ref, nw_ref, sc_ref, sh_ref, wr_ref, h_ref, route_ref, cnt_ref, carry):
    first = (pl.program_id(0) == 0) & (pl.program_id(1) == 0)

    @pl.when(first)
    def _():
        carry[...] = jnp.zeros_like(carry)

    h = _modulated_rmsnorm(x_ref[...], nw_ref[...], sc_ref[...], sh_ref[...])
    h_ref[...] = h
    tm = h.shape[0]
    i1, i2, g1, g2 = _top2(_dot_f32(h, wr_ref[...], passes=3))
    lane = lax.broadcasted_iota(jnp.int32, (tm, LANES), 1)
    oh1 = lane == i1
    oh2 = lane == i2
    both = jnp.where(oh1 | oh2, 1.0, 0.0)
    r = lax.broadcasted_iota(jnp.int32, (tm, tm), 0)
    c = lax.broadcasted_iota(jnp.int32, (tm, tm), 1)
    before = carry[...] + _dot((r > c).astype(BF16), both.astype(BF16))
    r1 = jnp.sum(jnp.where(oh1, before, 0.0), axis=-1, keepdims=True)
    r2 = jnp.sum(jnp.where(oh2, before, 0.0), axis=-1, keepdims=True)
    carry[...] += jnp.sum(both, axis=0, keepdims=True)
    cnt_ref[...] = carry[...]
    vals = (i1.astype(F32), i2.astype(F32), r1, r2, g1, g2)
    route = jnp.zeros((tm, LANES), F32)
    for k, v in enumerate(vals):
        route = jnp.where(lane == k, v, route)
    route_ref[...] = route


def _moe_route(x, mod, l, nw, wr, tm, per_row_mod):
    b, s, d = x.shape
    sc, sh, _ = _mod_specs(l, d, tm, per_row_mod, 2)
    tok = lambda w: pl.BlockSpec((None, tm, w), lambda i, t: (i, t, 0))
    return pl.pallas_call(
        _moe_route_kernel,
        out_shape=(jax.ShapeDtypeStruct((b, s, d), F32), jax.ShapeDtypeStruct((b, s, LANES), F32),
                   jax.ShapeDtypeStruct((1, LANES), F32)),
        grid=(b, s // tm),
        in_specs=[tok(d), pl.BlockSpec((None, 1, d), lambda i, t: (l, 0, 0)), sc, sh,
                  pl.BlockSpec((d, LANES), lambda i, t: (0, 0))],
        out_specs=(tok(d), tok(LANES), pl.BlockSpec((1, LANES), lambda i, t: (0, 0))),
        scratch_shapes=[pltpu.VMEM((1, LANES), F32)],
        compiler_params=_cparams(("arbitrary", "arbitrary")),
        name="moe_route",
    )(x, nw, mod, mod, wr)


def _row_copy(src, src_row, dst, dst_row, sem):
    return pltpu.make_async_copy(src.at[pl.ds(src_row, 1)], dst.at[pl.ds(dst_row, 1)], sem)


def _moe_dispatch_kernel(zrow_ref, d1_ref, d2_ref, h_ref, hs_ref, zbuf, sem, zsem):
    tm = h_ref.shape[0]
    tg = zbuf.shape[0]

    @pl.when(pl.program_id(0) == 0)
    def _():
        zbuf[...] = jnp.zeros_like(zbuf)
        fills = [pltpu.make_async_copy(
            zbuf, hs_ref.at[pl.ds(pl.multiple_of(jnp.maximum(zrow_ref[e], 0), tg), tg)], zsem)
            for e in range(N_EXPERTS)]
        for e in range(N_EXPERTS):
            @pl.when(zrow_ref[e] >= 0)
            def _():
                fills[e].start()
        for e in range(N_EXPERTS):
            @pl.when(zrow_ref[e] >= 0)
            def _():
                fills[e].wait()

        def tail(t):
            return pltpu.make_async_copy(zbuf, hs_ref.at[pl.ds(pl.multiple_of(t * tg, tg), tg)], zsem)

        n_tiles = hs_ref.shape[0] // tg
        lax.fori_loop(zrow_ref[N_EXPERTS], n_tiles, lambda t, c: (tail(t).start(), c)[1], 0)
        lax.fori_loop(zrow_ref[N_EXPERTS], n_tiles, lambda t, c: (tail(t).wait(), c)[1], 0)

    def start(r, _):
        _row_copy(h_ref, r, hs_ref, d1_ref[0, r], sem).start()
        _row_copy(h_ref, r, hs_ref, d2_ref[0, r], sem).start(priority=1)
        return 0

    def wait(r, _):
        _row_copy(h_ref, r, hs_ref, d1_ref[0, r], sem).wait()
        _row_copy(h_ref, r, hs_ref, d2_ref[0, r], sem).wait()
        return 0

    lax.fori_loop(0, tm, start, 0, unroll=8)
    lax.fori_loop(0, tm, wait, 0, unroll=8)


def _moe_dispatch(h, dest1, dest2, zrow, rows, tm, tg):
    n, d = h.shape
    idx = pl.BlockSpec((None, 1, tm), lambda i, z: (i, 0, 0), memory_space=pltpu.SMEM)
    grid_spec = pltpu.PrefetchScalarGridSpec(
        num_scalar_prefetch=1,
        grid=(n // tm,),
        in_specs=[idx, idx, pl.BlockSpec((tm, d), lambda i, z: (i, 0))],
        out_specs=pl.BlockSpec(memory_space=pl.ANY),
        scratch_shapes=[pltpu.VMEM((tg, d), F32), pltpu.SemaphoreType.DMA, pltpu.SemaphoreType.DMA],
    )
    return pl.pallas_call(
        _moe_dispatch_kernel,
        out_shape=jax.ShapeDtypeStruct((rows, d), F32),
        grid_spec=grid_spec,
        compiler_params=_cparams(("arbitrary",)),
        name="moe_dispatch",
    )(zrow, dest1, dest2, h)


def _moe_experts_kernel(te_ref, nv_ref, x_ref, wg_ref, wu_ref, wd_ref, o_ref):
    del te_ref
    i = pl.program_id(0)

    @pl.when(i < nv_ref[0])
    def _():
        o_ref[...] = _swiglu_chunks(x_ref[...].astype(BF16), wg_ref, wu_ref, wd_ref, chunk=3 * MXU_TILE)

    @pl.when(i >= nv_ref[0])
    def _():
        o_ref[...] = jnp.zeros_like(o_ref)


def _moe_experts(hs, tile_expert, n_valid, wg, wu, wd, tg):
    rows, d = hs.shape
    ff = wg.shape[-1]
    grid_spec = pltpu.PrefetchScalarGridSpec(
        num_scalar_prefetch=2,
        grid=(rows // tg,),
        in_specs=[pl.BlockSpec((tg, d), lambda i, te, nv: (i, 0)),
                  pl.BlockSpec((None, d, ff), lambda i, te, nv: (te[i], 0, 0)),
                  pl.BlockSpec((None, d, ff), lambda i, te, nv: (te[i], 0, 0)),
                  pl.BlockSpec((None, ff, d), lambda i, te, nv: (te[i], 0, 0))],
        out_specs=pl.BlockSpec((tg, d), lambda i, te, nv: (i, 0)),
    )
    return pl.pallas_call(
        _moe_experts_kernel,
        out_shape=jax.ShapeDtypeStruct((rows, d), F32),
        grid_spec=grid_spec,
        compiler_params=_cparams(("arbitrary",)),
        name="moe_experts",
    )(tile_expert, n_valid, hs, wg, wu, wd)


def _moe_combine_kernel(d1_ref, d2_ref, x_ref, g_ref, route_ref, fw_ref, ys_ref, o_ref, buf1, buf2, sem,
                        *, final):
    tm = x_ref.shape[0]

    def start(r, _):
        _row_copy(ys_ref, d1_ref[0, r], buf1, r, sem).start()
        _row_copy(ys_ref, d2_ref[0, r], buf2, r, sem).start(priority=1)
        return 0

    def wait(r, _):
        _row_copy(ys_ref, d1_ref[0, r], buf1, r, sem).wait()
        _row_copy(ys_ref, d2_ref[0, r], buf2, r, sem).wait()
        return 0

    lax.fori_loop(0, tm, start, 0, unroll=8)
    lax.fori_loop(0, tm, wait, 0, unroll=8)
    route = route_ref[...]
    f = route[:, _R_G1:_R_G1 + 1] * buf1[...] + route[:, _R_G2:_R_G2 + 1] * buf2[...]
    y = x_ref[...] + g_ref[...] * f
    o_ref[...] = _final_norm(y, fw_ref[...]) if final else y


def _moe_combine(x, mod, l, route, dest1, dest2, ys, fw, tm, per_row_mod, final):
    b, s, d = x.shape
    nt = s // tm
    _, _, g = _mod_specs(l, d, tm, per_row_mod, 2)
    idx = pl.BlockSpec((None, 1, tm), lambda i, t: (i * nt + t, 0, 0), memory_space=pltpu.SMEM)
    tok = lambda w: pl.BlockSpec((None, tm, w), lambda i, t: (i, t, 0))
    return pl.pallas_call(
        functools.partial(_moe_combine_kernel, final=final),
        out_shape=jax.ShapeDtypeStruct((b, s, d), F32),
        grid=(b, nt),
        in_specs=[idx, idx, tok(d), g, tok(LANES), pl.BlockSpec((1, d), lambda i, t: (0, 0)),
                  pl.BlockSpec(memory_space=pl.ANY)],
        out_specs=tok(d),
        scratch_shapes=[pltpu.VMEM((tm, d), F32), pltpu.VMEM((tm, d), F32), pltpu.SemaphoreType.DMA],
        compiler_params=_cparams(("arbitrary", "arbitrary")),
        name="moe_combine",
    )(dest1, dest2, x, mod, route, fw, ys)


def _moe(x, mod, l, nw, wr, wg, wu, wd, fw, tm, tg, per_row_mod, final):
    b, s, d = x.shape
    n = b * s
    h, route, counts = _moe_route(x, mod, l, nw, wr, tm, per_row_mod)
    cnt = counts[0, :N_EXPERTS].astype(jnp.int32)
    padded = (cnt + tg - 1) // tg * tg
    ends = jnp.cumsum(padded)
    offs = ends - padded
    rows = (2 * n + N_EXPERTS * (tg - 1) + tg - 1) // tg * tg
    n_valid = ends[-1] // tg
    tile = jnp.minimum(jnp.arange(rows // tg, dtype=jnp.int32), n_valid - 1)
    tile_expert = jnp.minimum(jnp.sum(tile[:, None] * tg >= ends[None, :], axis=1), N_EXPERTS - 1).astype(jnp.int32)
    rt = route.reshape(n, LANES)
    dest = lambda ke, kr: (offs[rt[:, ke].astype(jnp.int32)] + rt[:, kr].astype(jnp.int32)).reshape(n // tm, 1, tm)
    dest1, dest2 = dest(_R_E1, _R_R1), dest(_R_E2, _R_R2)
    zrow = jnp.concatenate([jnp.where(padded > 0, ends - tg, -1), n_valid[None]]).astype(jnp.int32)
    hs = _moe_dispatch(h.reshape(n, d), dest1, dest2, zrow, rows, tm, tg)
    ys = _moe_experts(hs, tile_expert, n_valid.reshape(1), wg, wu, wd, tg)
    return _moe_combine(x, mod, l, route, dest1, dest2, ys, fw, tm, per_row_mod, final)


def _page_cumsum_kernel(lf_ref, o_ref):
    n = lf_ref.shape[-1]
    r = lax.broadcasted_iota(jnp.int32, (n, n), 0)
    c = lax.broadcasted_iota(jnp.int32, (n, n), 1)
    o_ref[...] = _dot_tri(lf_ref[...], (r <= c).astype(BF16))


def _page_cumsum(lf_t):
    depth, h, n_pool, page = lf_t.shape
    return pl.pallas_call(
        _page_cumsum_kernel,
        out_shape=jax.ShapeDtypeStruct(lf_t.shape, F32),
        grid=(depth, h),
        in_specs=[pl.BlockSpec((None, None, n_pool, page), lambda l, i: (l, i, 0, 0))],
        out_specs=pl.BlockSpec((None, None, n_pool, page), lambda l, i: (l, i, 0, 0)),
        compiler_params=_cparams(("arbitrary", "arbitrary")),
        name="page_cumsum",
    )(lf_t)


def _fox_decode_kernel(pt_ref, q_ref, kn_ref, vn_ref, lfn_ref, k_hbm, v_hbm, f_hbm, o_ref,
                       pbuf, fbuf, psem, fsem, s_sc, acc, qb, *, l, nbuf):
    b = pl.program_id(0)
    n_pages = pt_ref.shape[1]
    page = pbuf.shape[-1]
    per_seq = 2 * n_pages
    total = pl.num_programs(0) * per_seq

    def page_copy(src, seq, j, slot):
        return pltpu.make_async_copy(src.at[l, pt_ref[seq, j]], pbuf.at[slot], psem.at[slot])

    def f_copy(seq, j, slot):
        pg = pt_ref[seq, j]
        src = f_hbm.at[l, :, lax.shift_right_logical(pg, 3), pl.ds(pg & 7, 1), :]
        return pltpu.make_async_copy(src, fbuf.at[slot], fsem.at[slot])

    def start_item(g, slot):
        seq = lax.div(g, per_seq)
        i = g - seq * per_seq

        @pl.when(i < n_pages)
        def _():
            page_copy(k_hbm, seq, i, slot).start()
            f_copy(seq, i, slot).start()

        @pl.when(i >= n_pages)
        def _():
            page_copy(v_hbm, seq, i - n_pages, slot).start()

    @pl.when(b == 0)
    def _():
        for s in range(nbuf):
            page_copy(k_hbm, 0, s, s).start()
            f_copy(0, s, s).start()

    q3 = q_ref[...].reshape(HEADS, HEAD_DIM, 1) * (HEAD_DIM ** -0.5)
    qb[...] = jnp.broadcast_to(q3, qb.shape)

    def k_body(i, run):
        slot = i & (nbuf - 1)
        page_copy(k_hbm, b, i, slot).wait()
        f_copy(b, i, slot).wait()
        floc = fbuf[slot]
        sc = jnp.sum(pbuf[slot] * qb[...], axis=1, keepdims=True)
        s_sc[:, :, pl.ds(pl.multiple_of(i * page, page), page)] = sc - (run + floc)
        nxt = b * per_seq + i + nbuf

        @pl.when(nxt < total)
        def _():
            start_item(nxt, slot)

        return run + floc[:, :, page - 1:page]

    run = lax.fori_loop(0, n_pages, k_body, jnp.zeros((HEADS, 1, 1), F32))

    s_all = s_sc[...]
    s_new = jnp.sum(q3 * kn_ref[...].reshape(HEADS, HEAD_DIM, 1), axis=1, keepdims=True) - (run + lfn_ref[...])
    m = jnp.maximum(jnp.max(s_all, axis=2, keepdims=True), s_new)
    p = jnp.exp(s_all - m)
    p_new = jnp.exp(s_new - m)
    s_sc[...] = p
    denom = jnp.sum(p, axis=2, keepdims=True) + p_new
    acc[...] = jnp.zeros_like(acc)

    def v_body(i, _):
        slot = i & (nbuf - 1)
        page_copy(v_hbm, b, i, slot).wait()
        acc[...] += pbuf[slot] * s_sc[:, :, pl.ds(pl.multiple_of(i * page, page), page)]
        nxt = b * per_seq + n_pages + i + nbuf

        @pl.when(nxt < total)
        def _():
            start_item(nxt, slot)

        return 0

    lax.fori_loop(0, n_pages, v_body, 0)
    o3 = jnp.sum(acc[...], axis=2, keepdims=True) + p_new * vn_ref[...].reshape(HEADS, HEAD_DIM, 1)
    o_ref[...] = (o3 / denom).reshape(MIX_W, 1)


def _fox_decode(page_table, q_col, kn_col, vn_col, lfn, k_t, v_t, floc, l, nbuf):
    nb, n_pages = page_table.shape
    page = k_t.shape[-1]
    assert nbuf <= n_pages and nbuf & (nbuf - 1) == 0 and n_pages % nbuf == 0
    col = pl.BlockSpec((None, MIX_W, 1), lambda i, pt: (i, 0, 0))
    hbm = pl.BlockSpec(memory_space=pl.ANY)
    grid_spec = pltpu.PrefetchScalarGridSpec(
        num_scalar_prefetch=1,
        grid=(nb,),
        in_specs=[col, col, col, pl.BlockSpec((None, HEADS, 1, 1), lambda i, pt: (i, 0, 0, 0)), hbm, hbm, hbm],
        out_specs=col,
        scratch_shapes=[pltpu.VMEM((nbuf, HEADS, HEAD_DIM, page), F32),
                        pltpu.VMEM((nbuf, HEADS, 1, page), F32),
                        pltpu.SemaphoreType.DMA((nbuf,)),
                        pltpu.SemaphoreType.DMA((nbuf,)),
                        pltpu.VMEM((HEADS, 1, n_pages * page), F32),
                        pltpu.VMEM((HEADS, HEAD_DIM, page), F32),
                        pltpu.VMEM((HEADS, HEAD_DIM, page), F32)],
    )
    return pl.pallas_call(
        functools.partial(_fox_decode_kernel, l=l, nbuf=nbuf),
        out_shape=jax.ShapeDtypeStruct((nb, MIX_W, 1), F32),
        grid_spec=grid_spec,
        compiler_params=_cparams(("arbitrary",)),
        name="fox_decode",
    )(page_table, q_col, kn_col, vn_col, lfn, k_t, v_t, floc)


def _mlstm_decode_kernel(q_ref, k_ref, v_ref, og_ref, li_ref, lf_ref, c_ref, n_ref, m_ref, nw_ref,
                         o_ref, c_out, n_out, m_out):
    q = q_ref[...]
    k = k_ref[...] * (HEAD_DIM ** -0.5)
    v = v_ref[...]
    c = c_ref[...]
    n = n_ref[...]
    m = m_ref[...]
    li = li_ref[...]
    lf = lf_ref[...]
    inter = lf + m
    m_t = jnp.maximum(inter, li)
    a_int = jnp.exp(inter - m_t)
    w_in = jnp.exp(li - m_t)
    s = jnp.sum(q * k, axis=1, keepdims=True) * w_in
    num = a_int * jnp.sum(q * c, axis=1, keepdims=True) + s * v
    den = a_int * jnp.sum(q * n, axis=1, keepdims=True) + s
    h = num / jnp.maximum(jnp.abs(den), jnp.exp(-m_t))
    h = h * lax.rsqrt(jnp.mean(h * h, axis=-1, keepdims=True) + EPS)
    o_ref[...] = _sigmoid(og_ref[...]) * (h * nw_ref[...])
    c_out[...] = a_int * c + w_in * (k * v)
    n_out[...] = a_int * n + w_in * k
    m_out[...] = m_t


def _mlstm_decode(q_col, k_col, v_row, og_row, li, lf, c_all, l, n0_col, m0, nw_row):
    nb = q_col.shape[0]
    d = HEAD_DIM
    c0spec = pl.BlockSpec((None, None, HEADS, d, d), lambda i: (l, i, 0, 0, 0))
    colspec = pl.BlockSpec((None, HEADS, d, 1), lambda i: (i, 0, 0, 0))
    rowspec = pl.BlockSpec((None, HEADS, 1, d), lambda i: (i, 0, 0, 0))
    sclspec = pl.BlockSpec((None, HEADS, 1, 1), lambda i: (i, 0, 0, 0))
    matspec = pl.BlockSpec((None, HEADS, d, d), lambda i: (i, 0, 0, 0))
    return pl.pallas_call(
        _mlstm_decode_kernel,
        out_shape=(jax.ShapeDtypeStruct((nb, HEADS, 1, d), F32),
                   jax.ShapeDtypeStruct((nb, HEADS, d, d), F32),
                   jax.ShapeDtypeStruct((nb, HEADS, d, 1), F32),
                   jax.ShapeDtypeStruct((nb, HEADS, 1, 1), F32)),
        grid=(nb,),
        in_specs=[colspec, colspec, rowspec, rowspec, sclspec, sclspec, c0spec, colspec, sclspec,
                  pl.BlockSpec((HEADS, 1, d), lambda i: (0, 0, 0))],
        out_specs=(rowspec, matspec, colspec, sclspec),
        compiler_params=_cparams(("arbitrary",)),
        name="mlstm_decode",
    )(q_col, k_col, v_row, og_row, li, lf, c_all, n0_col, m0, nw_row)


def _pool_decode_kernel(buf_ref, u_ref, wbd_ref, ps_ref, o_ref, nb_ref, *, pos):
    u = u_ref[...]
    n = buf_ref.shape[0]
    lane = lax.broadcasted_iota(jnp.int32, u.shape, 1)
    sums = {}
    run = u
    for back in range(1, max(POOL_WINDOWS)):
        run = run + buf_ref[n - back]
        if back + 1 in POOL_WINDOWS:
            sums[back + 1] = run
    cnt = {w: float(min(pos + 1, w)) for w in POOL_WINDOWS}
    mean = jnp.where(lane < 64, sums[2] / cnt[2], jnp.where(lane < 128, sums[4] / cnt[4],
                     jnp.where(lane < 192, sums[8] / cnt[8], sums[16] / cnt[16])))
    o_ref[...] = _mm(mean - u, wbd_ref[...]) * ps_ref[...]
    for r in range(n - 1):
        nb_ref[r] = buf_ref[r + 1]
    nb_ref[n - 1] = u


def _pool_decode(buf_t, u, wbd, pscale, l, pos):
    _, n, nb, w = buf_t.shape
    return pl.pallas_call(
        functools.partial(_pool_decode_kernel, pos=pos),
        out_shape=(jax.ShapeDtypeStruct((nb, w), F32), jax.ShapeDtypeStruct((n, nb, w), F32)),
        grid=(1,),
        in_specs=[pl.BlockSpec((None, n, nb, w), lambda i: (l, 0, 0, 0)),
                  pl.BlockSpec((nb, w), lambda i: (0, 0)),
                  pl.BlockSpec((None, w, w), lambda i: (l, 0, 0)),
                  pl.BlockSpec((None, 1, w), lambda i: (l, 0, 0))],
        out_specs=(pl.BlockSpec((nb, w), lambda i: (0, 0)),
                   pl.BlockSpec((n, nb, w), lambda i: (0, 0, 0))),
        compiler_params=_cparams(("arbitrary",)),
        name="pool_decode",
    )(buf_t, u, wbd, pscale)


def _tile(n, pref):
    t = min(n, pref)
    assert n % t == 0, (n, t)
    return t


def kernel(x_prompt, x_sample, c_prompt, c_sample, cache_fox_k, cache_fox_v, cache_fox_lf, page_table,
           state_mlstm_C, state_mlstm_n, state_mlstm_m, state_pool, w_ada, b_ada, norm1_w, norm2_w,
           w_in, b_fox_f, b_mlstm_i, b_mlstm_f, mlstm_norm_w, w_pool, pool_scale, w_out,
           w_ffn_gate, w_ffn_up, w_ffn_down, w_router, w_exp_gate, w_exp_up, w_exp_down, final_norm_w):
    depth = w_in.shape[0]
    b, s, d = x_prompt.shape
    nb = x_sample.shape[0]
    n_pool, page = cache_fox_k.shape[1], cache_fox_k.shape[2]
    assert x_sample.shape[1] == 1 and s % MLSTM_CHUNK == 0 and n_pool % 8 == 0

    pad_gate = lambda w: jnp.pad(w, ((0, 0), (0, 0), (0, LANES - HEADS)))
    w_in_f = jnp.concatenate(
        [w_in[:, :, :_MAIN_W], pad_gate(w_in[:, :, _MAIN_W:_MAIN_W + HEADS]),
         pad_gate(w_in[:, :, _MAIN_W + HEADS:_MAIN_W + 2 * HEADS]),
         pad_gate(w_in[:, :, _MAIN_W + 2 * HEADS:])], axis=-1)
    w_in_p = w_in_f.astype(BF16)
    gate_b = jnp.pad(jnp.stack([b_fox_f, b_mlstm_i, b_mlstm_f], axis=1), ((0, 0), (0, 0), (0, LANES - HEADS)))
    wbd_f = jnp.zeros((depth, POOL_W, POOL_W), F32)
    for g in range(len(POOL_WINDOWS)):
        sl = slice(g * HEAD_DIM, (g + 1) * HEAD_DIM)
        wbd_f = wbd_f.at[:, sl, sl].set(w_pool[:, g])
    wbd = wbd_f.astype(BF16)
    pscale = pool_scale.reshape(depth, 1, POOL_W)
    w_out_b = w_out.astype(BF16)
    nw1 = norm1_w.reshape(depth, 1, d)
    nw2 = norm2_w.reshape(depth, 1, d)
    mnw = mlstm_norm_w.reshape(depth, 1, MIX_W)
    mnw_row = mlstm_norm_w.reshape(depth, HEADS, 1, HEAD_DIM)
    fw = final_norm_w.reshape(1, d)
    w_router_p = jnp.pad(w_router, ((0, 0), (0, 0), (0, LANES - N_EXPERTS)))
    wfg, wfu, wfd = (w.astype(BF16) for w in (w_ffn_gate, w_ffn_up, w_ffn_down))
    weg, weu, wed = (w.astype(BF16) for w in (w_exp_gate, w_exp_up, w_exp_down))

    k_t = jnp.transpose(cache_fox_k, (0, 1, 3, 4, 2))
    v_t = jnp.transpose(cache_fox_v, (0, 1, 3, 4, 2))
    lf_t = jnp.transpose(cache_fox_lf, (0, 3, 1, 2))
    floc = _page_cumsum(lf_t).reshape(depth, HEADS, n_pool // 8, 8, page)
    pool_t = jnp.transpose(state_pool, (0, 2, 1, 3))

    mod = _ada_modulation(jnp.concatenate([c_prompt, c_sample], axis=0), w_ada, b_ada)
    mod_p = mod[:, :, :b].reshape(depth, 6, b, 1, d)
    mod_s = mod[:, :, b:]

    tm = _tile(s, 512)
    tq = _tile(s, 1024)
    tf = w_ffn_gate.shape[-1]
    xp = x_prompt
    xs = x_sample.reshape(1, nb, d)
    st_p, st_s = [], []
    for l in range(depth):
        last = l == depth - 1
        (qa, ka, va, kf, vf, lff, qm, km, vm, om, li, lfm, opool, ulast) = _in_proj_prompt(
            xp, mod_p, l, nw1, w_in_p, gate_b, wbd, pscale, tm)
        o_fox = _fox_prompt(qa, ka, va, tq)
        o_m, c_pair, n_pair, m_fin = _mlstm_prompt(qm, km, vm, om, li, lfm, mnw, l, 4 if b % 4 == 0 else 1)
        x1 = _out_proj(o_fox, o_m, opool, xp, mod_p, l, w_out_b, tm, False)
        if l % 2 == 0:
            j = l // 2
            xp = _ffn(x1, mod_p, l, nw2, wfg[j], wfu[j], wfd[j], fw, tm, tf, False, last)
        else:
            j = l // 2
            xp = _moe(x1, mod_p, l, nw2, w_router_p[j], weg[j], weu[j], wed[j], fw, tm, 256, False, last)
        c_fin = jnp.stack([c_pair[:, p, e * HEAD_DIM:(e + 1) * HEAD_DIM, e * HEAD_DIM:(e + 1) * HEAD_DIM]
                           for p in range(HEADS // 2) for e in range(2)], axis=1)
        st_p.append((kf.reshape(b, s, HEADS, HEAD_DIM), vf.reshape(b, s, HEADS, HEAD_DIM), lff[:, :, :HEADS],
                     c_fin, n_pair.reshape(b, HEADS, HEAD_DIM), m_fin[:, 0, :HEADS],
                     ulast[:, POOL_HALO - POOL_BUF:]))

        z, lff_s, li_s, lfm_s = _in_proj_sample(xs[0], mod_s, l, nw1, w_in_f, gate_b)
        seg = lambda i: z[:, i * MIX_W:(i + 1) * MIX_W]
        qf_s, kf_s, vf_s, qm_s, km_s, vm_s, om_s = (seg(i) for i in range(7))
        u_s = z[:, 7 * MIX_W:]
        col = lambda a: a.reshape(nb, MIX_W, 1)
        o_fox_s = _fox_decode(page_table, col(qf_s), col(kf_s), col(vf_s),
                              lff_s[:, :HEADS].reshape(nb, HEADS, 1, 1), k_t, v_t, floc, l,
                              min(64, page_table.shape[1]))
        hcol = lambda a: a.reshape(nb, HEADS, HEAD_DIM, 1)
        hrow = lambda a: a.reshape(nb, HEADS, 1, HEAD_DIM)
        hscl = lambda a: a[:, :HEADS].reshape(nb, HEADS, 1, 1)
        o_m_s, c_new, n_new, m_new = _mlstm_decode(
            hcol(qm_s), hcol(km_s), hrow(vm_s), hrow(om_s), hscl(li_s), hscl(lfm_s),
            state_mlstm_C, l, hcol(state_mlstm_n[l]), state_mlstm_m[l].reshape(nb, HEADS, 1, 1), mnw_row[l])
        o_pool_s, buf_new = _pool_decode(pool_t, u_s, wbd_f, pscale, l, page_table.shape[1] * page)
        mix = lambda a: a.reshape(1, nb, -1)
        x1s = _out_proj(mix(o_fox_s), mix(o_m_s), mix(o_pool_s), xs, mod_s, l, w_out, nb, True)
        if l % 2 == 0:
            xs = _ffn(x1s, mod_s, l, nw2, w_ffn_gate[j], w_ffn_up[j], w_ffn_down[j], fw, nb, 2 * LANES, True, last)
        else:
            xs = _moe(x1s, mod_s, l, nw2, w_router_p[j], weg[j], weu[j], wed[j], fw, nb, 8, True, last)
        st_s.append((kf_s.reshape(nb, 1, HEADS, HEAD_DIM), vf_s.reshape(nb, 1, HEADS, HEAD_DIM),
                     lff_s[:, :HEADS].reshape(nb, 1, HEADS), c_new, n_new.reshape(nb, HEADS, HEAD_DIM),
                     m_new.reshape(nb, HEADS), jnp.transpose(buf_new, (1, 0, 2))))

    outs_p = [jnp.stack(a) for a in zip(*st_p)]
    outs_s = [jnp.stack(a) for a in zip(*st_s)]
    return (xp, xs.reshape(nb, 1, d), *outs_p, *outs_s)
```
